```python
import math
import jax, jax.numpy as jnp
from jax import lax
import numpy as np

D_MODEL = 1024
BATCH = 32
SEQ = 256
DEPTH = 2
DEC_BATCH = 4
DEC_SEQ = 4096
PAST_LEN = 256

GRID_W = 64
MIX_W = D_MODEL
POOL_W = MIX_W // 4
POOL_WINDOWS = (2, 4, 8, 16)
POOL_GROUP = POOL_W // 4
ATT_W = MIX_W // 4
N_ATT_HEADS = 4
ATT_V_DIM = ATT_W // N_ATT_HEADS
ATT_QK_DIM = ATT_V_DIM // 2
QK_W = N_ATT_HEADS * 2 * ATT_QK_DIM
ROPE_FREQS = ATT_QK_DIM // 4
ROPE_THETA = 10000.0
Q_BLOCK = 128
SSM_W = MIX_W - POOL_W - ATT_W
SSM_HEAD_DIM = 64
N_SSM_HEADS = SSM_W // SSM_HEAD_DIM
SSM_GROUPS = 2
SSM_STATE = 64
SSM_CONV = 5
SSM_CHUNK = 128
N_DIRS = 2
CONV_CH = SSM_W + 2 * SSM_GROUPS * SSM_STATE
N_EXPERTS = 64
N_EXPERT_GROUPS = 8
TOPK_GROUPS = 4
TOP_K = 8
EXPERT_FF = D_MODEL // 4
ROUTED_SCALE = 2.5
MOE_BLOCK = 128
EPS = 1e-6
IN_SPLITS = (POOL_W, POOL_W + QK_W, POOL_W + 2 * QK_W, POOL_W + 2 * QK_W + ATT_W, POOL_W + 2 * QK_W + ATT_W + SSM_W, POOL_W + 2 * QK_W + ATT_W + SSM_W + CONV_CH)
IN_W = POOL_W + 2 * QK_W + ATT_W + SSM_W + CONV_CH + N_DIRS * N_SSM_HEADS

kernel_name = 'hybrid_pool_diffattn_ssd_moe_step'

F32 = jnp.float32


def rmsnorm(x, g):
    xf = x.astype(F32)
    y = xf * lax.rsqrt(jnp.mean(xf * xf, axis=-1, keepdims=True) + EPS)
    return (y * g.astype(F32)).astype(x.dtype)


def pool_mixer(u, w, scale):
    b, t, _ = u.shape
    uf = u.astype(F32)
    cs = jnp.concatenate([jnp.zeros((b, 1, POOL_W), F32), jnp.cumsum(uf, axis=1)], axis=1)
    pos = jnp.arange(t)
    outs = []
    for gi, win in enumerate(POOL_WINDOWS):
        sl = slice(gi * POOL_GROUP, (gi + 1) * POOL_GROUP)
        lo = jnp.clip(pos - win // 2, 0, t - 1)
        hi = jnp.clip(pos + (win - 1 - win // 2), 0, t - 1)
        csg = cs[..., sl]
        cnt = (hi - lo + 1).astype(F32)[None, :, None]
        outs.append((csg[:, hi + 1] - csg[:, lo]) / cnt - uf[..., sl])
    pooled = jnp.stack(outs, axis=2)
    mixed = jnp.einsum('btgi,gio->btgo', pooled, w.astype(F32))
    return (mixed.reshape(b, t, POOL_W) * scale.astype(F32)).astype(u.dtype)


def axial_rope_tables(n_tokens):
    rows = n_tokens // GRID_W
    r, col = jnp.meshgrid(jnp.arange(rows), jnp.arange(GRID_W), indexing='ij')
    pos = jnp.stack([r.reshape(-1), col.reshape(-1)], axis=-1).astype(F32)
    inv = ROPE_THETA ** (-jnp.arange(ROPE_FREQS, dtype=F32) / ROPE_FREQS)
    ang = pos[:, :, None] * inv
    return jnp.cos(ang), jnp.sin(ang)


def apply_axial_rope(x, cos, sin):
    shp = x.shape
    xr = x.astype(F32).reshape(shp[:-1] + (2, 2, ROPE_FREQS))
    x1, x2 = xr[..., 0, :], xr[..., 1, :]
    cs = cos[None, :, None, None]
    sn = sin[None, :, None, None]
    out = jnp.stack([x1 * cs - x2 * sn, x2 * cs + x1 * sn], axis=-2)
    return out.reshape(shp).astype(x.dtype)


def diff_attention(q, k, v, lam):
    b, t, h, _, dq = q.shape
    nb = t // Q_BLOCK
    scale = dq ** -0.5
    vf = v.astype(F32)
    qb = jnp.moveaxis(q.reshape(b, nb, Q_BLOCK, h, 2, dq), 1, 0)

    def block(qblk):
        s = jnp.einsum('bqhcd,bkhcd->bhcqk', qblk, k).astype(F32) * scale
        a = jax.nn.softmax(s, axis=-1)
        w = a[:, :, 0] - lam * a[:, :, 1]
        return jnp.einsum('bhqk,bkhd->bqhd', w, vf)

    out = lax.map(block, qb)
    return jnp.moveaxis(out, 0, 1).reshape(b, t, h, v.shape[-1]).astype(v.dtype)


def depthwise_conv(u, w, bias):
    out = lax.conv_general_dilated(u, w[:, None, :], window_strides=(1,), padding=[(SSM_CONV // 2, SSM_CONV // 2)], dimension_numbers=('NWC', 'WIO', 'NWC'), feature_group_count=u.shape[-1])
    return out + bias


def segsum(a):
    n = a.shape[-1]
    ar = jnp.broadcast_to(a[..., None], a.shape + (n,))
    strict = jnp.tril(jnp.ones((n, n), bool), -1)
    cs = jnp.cumsum(jnp.where(strict, ar, 0.0), axis=-2)
    return jnp.where(jnp.tril(jnp.ones((n, n), bool)), cs, -jnp.inf)


def ssd_scan(x, dt, a_coef, bm, cm, h0):
    b, l, h, p = x.shape
    n = bm.shape[-1]
    nc = l // SSM_CHUNK
    xd = (x * dt[..., None]).reshape(b, nc, SSM_CHUNK, h, p)
    a = jnp.moveaxis((dt * a_coef).reshape(b, nc, SSM_CHUNK, h), 3, 1)
    bc = bm.reshape(b, nc, SSM_CHUNK, h, n)
    cc = cm.reshape(b, nc, SSM_CHUNK, h, n)
    a_cs = jnp.cumsum(a, axis=-1)
    scores = jnp.einsum('bclhn,bcshn->bhcls', cc, bc) * jnp.exp(segsum(a))
    y_diag = jnp.einsum('bhcls,bcshp->bclhp', scores, xd)
    decay_in = jnp.moveaxis(jnp.exp(a_cs[..., -1:] - a_cs), 1, 3)
    states = jnp.einsum('bclhn,bclhp->bchpn', bc * decay_in[..., None], xd)
    states = jnp.concatenate([h0[:, None], states], axis=1)
    chunk_decay = jnp.exp(segsum(jnp.pad(a_cs[..., -1], ((0, 0), (0, 0), (1, 0)))))
    states = jnp.einsum('bhzc,bchpn->bzhpn', chunk_decay, states)
    h_final = states[:, -1]
    decay_out = jnp.moveaxis(jnp.exp(a_cs), 1, 3)
    y_off = jnp.einsum('bclhn,bchpn->bclhp', cc, states[:, :-1]) * decay_out[..., None]
    return (y_diag + y_off).reshape(b, l, h, p), h_final


def swiglu(x, wg, wu, wd):
    return (jax.nn.silu(x @ wg) * (x @ wu)) @ wd


def routed_experts(xt, eidx, wts, w_gate, w_up, w_down):
    n, d = xt.shape
    s = n * TOP_K
    flat_e = eidx.reshape(-1)
    flat_tok = jnp.arange(s, dtype=jnp.int32) // TOP_K
    flat_w = wts.reshape(-1)
    order = jnp.argsort(flat_e)
    se = flat_e[order]
    counts = jnp.bincount(flat_e, length=N_EXPERTS)
    padded = (counts + MOE_BLOCK - 1) // MOE_BLOCK * MOE_BLOCK
    pend = jnp.cumsum(padded)
    pstart = pend - padded
    ustart = jnp.cumsum(counts) - counts
    dest = pstart[se] + jnp.arange(s, dtype=jnp.int32) - ustart[se]
    nblk = (s + N_EXPERTS * (MOE_BLOCK - 1) + MOE_BLOCK - 1) // MOE_BLOCK
    rows = nblk * MOE_BLOCK
    row_tok = jnp.full((rows,), n, jnp.int32).at[dest].set(flat_tok[order])
    row_w = jnp.zeros((rows,), F32).at[dest].set(flat_w[order])
    blk_e = jnp.minimum(jnp.searchsorted(pend, jnp.arange(nblk, dtype=jnp.int32) * MOE_BLOCK, side='right'), N_EXPERTS - 1)
    x_pad = jnp.concatenate([xt, jnp.zeros((1, d), xt.dtype)], axis=0)

    def run(args):
        tok, e = args
        return swiglu(x_pad[tok], w_gate[e], w_up[e], w_down[e])

    y_rows = lax.map(run, (row_tok.reshape(nblk, MOE_BLOCK), blk_e))
    y = jnp.zeros((n + 1, d), F32).at[row_tok].add(y_rows.reshape(rows, d).astype(F32) * row_w[:, None])
    return y[:n]


def moe_ffn(h, p):
    b, t, d = h.shape
    xt = h.reshape(b * t, d)
    n = xt.shape[0]
    per_group = N_EXPERTS // N_EXPERT_GROUPS
    scores = jax.nn.sigmoid((xt @ p['w_router']).astype(F32))
    sel = scores + p['router_bias'].astype(F32)
    group_score = lax.top_k(sel.reshape(n, N_EXPERT_GROUPS, per_group), 2)[0].sum(-1)
    _, gidx = lax.top_k(group_score, TOPK_GROUPS)
    gmask = jax.nn.one_hot(gidx, N_EXPERT_GROUPS, dtype=F32).sum(1) > 0
    sel = jnp.where(jnp.repeat(gmask, per_group, axis=1), sel, -jnp.inf)
    _, eidx = lax.top_k(sel, TOP_K)
    wts = jnp.take_along_axis(scores, eidx, axis=1)
    wts = wts / jnp.sum(wts, axis=-1, keepdims=True) * ROUTED_SCALE
    routed = routed_experts(xt, eidx, wts, p['w_gate'], p['w_up'], p['w_down'])
    shared = swiglu(xt, p['ws_gate'], p['ws_up'], p['ws_down']).astype(F32)
    return (routed + shared).reshape(b, t, d)


def trunk_layer(x, cond, p, layer_idx, ctx):
    b, t, _ = x.shape
    dtype = x.dtype
    mod = (jax.nn.silu(cond) @ p['w_mod'] + p['b_mod'])[:, None, :]
    sh1, sc1, g1, sh2, sc2, g2 = jnp.split(mod, 6, axis=-1)
    h = rmsnorm(x, p['norm1_g']) * (1 + sc1) + sh1
    u = h @ p['w_in']
    pool_in, q, k, v, z, xbc, dt_raw = jnp.split(u, IN_SPLITS, axis=-1)

    pool_out = pool_mixer(pool_in, p['pool_w'], p['pool_scale'])

    q = rmsnorm(q.reshape(b, t, N_ATT_HEADS, 2, ATT_QK_DIM), p['q_norm_g'])
    k = rmsnorm(k.reshape(b, t, N_ATT_HEADS, 2, ATT_QK_DIM), p['k_norm_g'])
    v = v.reshape(b, t, N_ATT_HEADS, ATT_V_DIM)
    if ctx is None:
        q_r, k_all, v_all = q, k, v
    else:
        cos, sin = axial_rope_tables(t)
        q_r = apply_axial_rope(q, cos, sin)
        k_all = jnp.concatenate([apply_axial_rope(k, cos, sin), ctx[0].astype(dtype)], axis=1)
        v_all = jnp.concatenate([v, ctx[1].astype(dtype)], axis=1)
    lam_init = 0.8 - 0.6 * math.exp(-0.3 * layer_idx)
    lq = p['lambda_q'].astype(F32)
    lk = p['lambda_k'].astype(F32)
    lam = jnp.exp(jnp.sum(lq[0] * lk[0])) - jnp.exp(jnp.sum(lq[1] * lk[1])) + lam_init
    att = diff_attention(q_r, k_all, v_all, lam)
    att = (rmsnorm(att, p['subln_g']) * (1.0 - lam_init)).reshape(b, t, ATT_W)

    xbc = jax.nn.silu(depthwise_conv(xbc, p['conv_w'], p['conv_b'])).astype(F32)
    gn = SSM_GROUPS * SSM_STATE
    rep = N_SSM_HEADS // SSM_GROUPS
    xs = xbc[..., :SSM_W].reshape(b, t, N_SSM_HEADS, SSM_HEAD_DIM)
    bm = jnp.repeat(xbc[..., SSM_W:SSM_W + gn].reshape(b, t, SSM_GROUPS, SSM_STATE), rep, axis=2)
    cm = jnp.repeat(xbc[..., SSM_W + gn:].reshape(b, t, SSM_GROUPS, SSM_STATE), rep, axis=2)
    dt = jax.nn.softplus(dt_raw.astype(F32).reshape(b, t, N_DIRS, N_SSM_HEADS) + p['dt_bias'].astype(F32))
    a_coef = -jnp.exp(p['a_log'].astype(F32))
    if ctx is None:
        h0 = jnp.zeros((b, N_DIRS, N_SSM_HEADS, SSM_HEAD_DIM, SSM_STATE), F32)
    else:
        h0 = ctx[2].astype(F32)
    y_f, h_f = ssd_scan(xs, dt[:, :, 0], a_coef[0], bm, cm, h0[:, 0])
    y_b, h_b = ssd_scan(xs[:, ::-1], dt[:, ::-1, 1], a_coef[1], bm[:, ::-1], cm[:, ::-1], h0[:, 1])
    y = y_f + y_b[:, ::-1] + xs * p['d_skip'].astype(F32)[:, None]
    y = y.reshape(b, t, SSM_W) * jax.nn.silu(z.astype(F32))
    ssm_out = rmsnorm(y.reshape(b, t, SSM_GROUPS, SSM_W // SSM_GROUPS), p['ssm_norm_g'].reshape(SSM_GROUPS, SSM_W // SSM_GROUPS)).reshape(b, t, SSM_W).astype(dtype)

    mixed = jnp.concatenate([pool_out, att, ssm_out], axis=-1) @ p['w_out']
    x = x + (g1 * mixed).astype(dtype)
    h2 = rmsnorm(x, p['norm2_g']) * (1 + sc2) + sh2
    x = x + (g2 * moe_ffn(h2, p)).astype(dtype)
    return x, (k, v, jnp.stack([h_f, h_b], axis=1).astype(dtype))


def setup_inputs(seed: int = 0) -> dict:
    key = jax.random.key(seed)
    ks = iter(jax.random.split(key, 40))
    L = DEPTH

    def nrm(shape, scale=1.0):
        return jax.random.normal(next(ks), shape, F32) * scale

    def gain(shape):
        return 1.0 + 0.1 * jax.random.normal(next(ks), shape, F32)

    x_prompt = nrm((BATCH, SEQ, D_MODEL))
    x_sample = nrm((DEC_BATCH, DEC_SEQ, D_MODEL))
    cache_k = nrm((DEC_BATCH, DEPTH, PAST_LEN, N_ATT_HEADS, 2, ATT_QK_DIM))
    cache_v = nrm((DEC_BATCH, DEPTH, PAST_LEN, N_ATT_HEADS, ATT_V_DIM))
    state_ssm = nrm((DEC_BATCH, DEPTH, N_DIRS, N_SSM_HEADS, SSM_HEAD_DIM, SSM_STATE), 0.5)
    c = nrm((DEC_BATCH, D_MODEL))
    c_ctx = nrm((D_MODEL,))
    dt0 = jnp.exp(jax.random.uniform(next(ks), (L, N_DIRS, N_SSM_HEADS), F32, math.log(1e-3), math.log(1e-1)))
    a_log = jnp.log(jax.random.uniform(next(ks), (L, N_DIRS, N_SSM_HEADS), F32, 1.0, 16.0))
    return {
        'x_prompt': x_prompt,
        'x_sample': x_sample,
        'cache_k': cache_k,
        'cache_v': cache_v,
        'state_ssm': state_ssm,
        'c': c,
        'c_ctx': c_ctx,
        'w_mod': nrm((L, D_MODEL, 6 * D_MODEL), 0.5 * D_MODEL ** -0.5),
        'b_mod': nrm((L, 6 * D_MODEL), 0.01),
        'norm1_g': gain((L, D_MODEL)),
        'w_in': nrm((L, D_MODEL, IN_W), D_MODEL ** -0.5),
        'pool_w': nrm((L, len(POOL_WINDOWS), POOL_GROUP, POOL_GROUP), POOL_GROUP ** -0.5),
        'pool_scale': gain((L, POOL_W)),
        'q_norm_g': gain((L, ATT_QK_DIM)),
        'k_norm_g': gain((L, ATT_QK_DIM)),
        'lambda_q': nrm((L, 2, ATT_QK_DIM), 0.1),
        'lambda_k': nrm((L, 2, ATT_QK_DIM), 0.1),
        'subln_g': gain((L, ATT_V_DIM)),
        'conv_w': nrm((L, SSM_CONV, CONV_CH), SSM_CONV ** -0.5),
        'conv_b': nrm((L, CONV_CH), 0.01),
        'dt_bias': dt0 + jnp.log(-jnp.expm1(-dt0)),
        'a_log': a_log,
        'd_skip': gain((L, N_SSM_HEADS)),
        'ssm_norm_g': gain((L, SSM_W)),
        'w_out': nrm((L, MIX_W, D_MODEL), MIX_W ** -0.5),
        'norm2_g': gain((L, D_MODEL)),
        'w_router': nrm((L, D_MODEL, N_EXPERTS), D_MODEL ** -0.5),
        'router_bias': nrm((L, N_EXPERTS), 0.01),
        'w_gate': nrm((L, N_EXPERTS, D_MODEL, EXPERT_FF), D_MODEL ** -0.5),
        'w_up': nrm((L, N_EXPERTS, D_MODEL, EXPERT_FF), D_MODEL ** -0.5),
        'w_down': nrm((L, N_EXPERTS, EXPERT_FF, D_MODEL), EXPERT_FF ** -0.5),
        'ws_gate': nrm((L, D_MODEL, EXPERT_FF), D_MODEL ** -0.5),
        'ws_up': nrm((L, D_MODEL, EXPERT_FF), D_MODEL ** -0.5),
        'ws_down': nrm((L, EXPERT_FF, D_MODEL), EXPERT_FF ** -0.5),
    }


def reference(x_prompt, x_sample, cache_k, cache_v, state_ssm, c, c_ctx, w_mod, b_mod, norm1_g, w_in, pool_w, pool_scale, q_norm_g, k_norm_g, lambda_q, lambda_k, subln_g, conv_w, conv_b, dt_bias, a_log, d_skip, ssm_norm_g, w_out, norm2_g, w_router, router_bias, w_gate, w_up, w_down, ws_gate, ws_up, ws_down):
    stacked = {'w_mod': w_mod, 'b_mod': b_mod, 'norm1_g': norm1_g, 'w_in': w_in, 'pool_w': pool_w, 'pool_scale': pool_scale, 'q_norm_g': q_norm_g, 'k_norm_g': k_norm_g, 'lambda_q': lambda_q, 'lambda_k': lambda_k, 'subln_g': subln_g, 'conv_w': conv_w, 'conv_b': conv_b, 'dt_bias': dt_bias, 'a_log': a_log, 'd_skip': d_skip, 'ssm_norm_g': ssm_norm_g, 'w_out': w_out, 'norm2_g': norm2_g, 'w_router': w_router, 'router_bias': router_bias, 'w_gate': w_gate, 'w_up': w_up, 'w_down': w_down, 'ws_gate': ws_gate, 'ws_up': ws_up, 'ws_down': ws_down}

    y_prompt = x_prompt
    ks, vs, hs = [], [], []
    for l in range(DEPTH):
        p = {name: arr[l] for name, arr in stacked.items()}
        y_prompt, (k_l, v_l, h_l) = trunk_layer(y_prompt, c_ctx[None], p, l, None)
        ks.append(k_l)
        vs.append(v_l)
        hs.append(h_l)

    y_sample = x_sample
    for l in range(DEPTH):
        p = {name: arr[l] for name, arr in stacked.items()}
        y_sample, _ = trunk_layer(y_sample, c, p, l, (cache_k[:, l], cache_v[:, l], state_ssm[:, l]))

    new_k = jnp.stack(ks, axis=1)
    new_v = jnp.stack(vs, axis=1)
    new_ssm = jnp.stack(hs, axis=1)
    return (y_prompt, y_sample, new_k, new_v, new_ssm)
```

```python
import functools
import math

import jax
import jax.numpy as jnp
from jax import lax
from jax.experimental import pallas as pl
from jax.experimental.pallas import tpu as pltpu

F32 = jnp.float32
BF16 = jnp.bfloat16

D_MODEL = 1024
GRID_W = 64
POOL_W = 256
POOL_WINDOWS = (2, 4, 8, 16)
POOL_GROUP = 64
N_ATT_HEADS = 4
ATT_V_DIM = 64
ATT_QK_DIM = 32
ATT_W = 256
QK_W = 256
ROPE_FREQS = 8
ROPE_THETA = 10000.0
SSM_W = 512
SSM_HEAD_DIM = 64
N_SSM_HEADS = 8
SSM_GROUPS = 2
SSM_STATE = 64
SSM_CONV = 5
SSM_CHUNK = 128
N_DIRS = 2
CONV_CH = 768
N_EXPERTS = 64
N_EXPERT_GROUPS = 8
TOPK_GROUPS = 4
TOP_K = 8
EXPERT_FF = 256
ROUTED_SCALE = 2.5
EPS = 1e-6
IN_W = 2320
IN_W_PAD = 2432

LANES = 128
SUBLANES = 8
TM = 256
HALO = 8
MOE_TS = 2048
MOE_BLK = 128
VMEM_LIMIT = 48 * 1024 * 1024

HIGHEST = lax.Precision.HIGHEST


def _cparams(sem):
    return pltpu.CompilerParams(dimension_semantics=sem, vmem_limit_bytes=VMEM_LIMIT)


def _silu(x):
    return x * jax.nn.sigmoid(x)


def _softplus(x):
    return jnp.maximum(x, 0.0) + jnp.log1p(jnp.exp(-jnp.abs(x)))


def _mod_kernel(c_ref, w_ref, b_ref, o_ref):
    c = c_ref[...]
    o_ref[0] = jnp.dot(_silu(c), w_ref[0], preferred_element_type=F32, precision=HIGHEST) + b_ref[0]


def _modulation(cond, w_mod, b_mod):
    L = w_mod.shape[0]
    bn = 1536
    return pl.pallas_call(
        _mod_kernel,
        grid=(L, 6 * D_MODEL // bn),
        in_specs=[
            pl.BlockSpec((SUBLANES, D_MODEL), lambda l, j: (0, 0)),
            pl.BlockSpec((1, D_MODEL, bn), lambda l, j: (l, 0, j)),
            pl.BlockSpec((1, 1, bn), lambda l, j: (l, 0, j)),
        ],
        out_specs=pl.BlockSpec((1, SUBLANES, bn), lambda l, j: (l, 0, j)),
        out_shape=jax.ShapeDtypeStruct((L, SUBLANES, 6 * D_MODEL), F32),
        compiler_params=_cparams(("parallel", "parallel")),
        name="adaln_mod",
    )(cond, w_mod, b_mod.reshape(L, 1, 6 * D_MODEL))


def _inproj_kernel(x_ref, mod_ref, g_ref, w_ref, wdt_ref, qg_ref, kg_ref, cos_ref, sin_ref, ones_ref,
                   pool_ref, qn_ref, kT_ref, kn_ref, v_ref, z_ref, xbc_ref, dt_ref, dtT_ref):
    x = x_ref[...]
    xn = x * lax.rsqrt(jnp.mean(x * x, axis=-1, keepdims=True) + EPS) * g_ref[...]
    h = xn * (1.0 + mod_ref[0, 1:2, :]) + mod_ref[0, 0:1, :]
    hb = h.astype(BF16)
    u = jnp.dot(hb, w_ref[...], preferred_element_type=F32)
    pool_ref[...] = u[:, 0:256]
    v_ref[...] = u[:, 768:1024]
    z_ref[...] = u[:, 1024:1536]
    xbc_ref[...] = u[:, 1536:2304]
    dt_ref[...] = u[:, 2304:2304 + LANES][:, 0:16]
    dtT_ref[...] = lax.dot_general(wdt_ref[...], hb, (((1,), (1,)), ((), ())), preferred_element_type=F32)

    lane = lax.broadcasted_iota(jnp.int32, (TM, QK_W), 1)
    first_half = (lane % ATT_QK_DIM) < (ATT_QK_DIM // 2)
    cos = cos_ref[...]
    sin = sin_ref[...]

    def qk_norm(t, gain):
        ss = jnp.dot(t * t, ones_ref[...], preferred_element_type=F32, precision=HIGHEST)
        return t * lax.rsqrt(ss * (1.0 / ATT_QK_DIM) + EPS) * gain

    def rope(t):
        partner = jnp.where(first_half, pltpu.roll(t, QK_W - ATT_QK_DIM // 2, 1), pltpu.roll(t, ATT_QK_DIM // 2, 1))
        return t * cos + partner * sin

    qn_ref[...] = rope(qk_norm(u[:, 256:512], qg_ref[...])).astype(BF16)
    kn = qk_norm(u[:, 512:768], kg_ref[...])
    kn_ref[...] = kn
    kT_ref[...] = rope(kn).T.astype(BF16)


def _in_projection(x, mod_l, norm_g, w_in_p, w_dtT, qg, kg, cos_t, sin_t, ones_bd, n_ctx_tiles, lat_tiles):
    n = x.shape[0]
    nt = n // TM

    def mod_row(i):
        return jnp.where(i < n_ctx_tiles, 0, 1 + (i - n_ctx_tiles) // lat_tiles)

    def rope_row(i):
        return jnp.where(i < n_ctx_tiles, lat_tiles, (i - n_ctx_tiles) % lat_tiles)

    row = lambda w: pl.BlockSpec((TM, w), lambda i: (i, 0))
    const = lambda a: pl.BlockSpec(a.shape, lambda i: (0,) * a.ndim)
    outs = [
        jax.ShapeDtypeStruct((n, POOL_W), F32),
        jax.ShapeDtypeStruct((n, QK_W), BF16),
        jax.ShapeDtypeStruct((QK_W, n), BF16),
        jax.ShapeDtypeStruct((n, QK_W), F32),
        jax.ShapeDtypeStruct((n, ATT_W), F32),
        jax.ShapeDtypeStruct((n, SSM_W), F32),
        jax.ShapeDtypeStruct((n, CONV_CH), F32),
        jax.ShapeDtypeStruct((n, 16), F32),
        jax.ShapeDtypeStruct((16, n), F32),
    ]
    return pl.pallas_call(
        _inproj_kernel,
        grid=(nt,),
        in_specs=[
            row(D_MODEL),
            pl.BlockSpec((1, 6, D_MODEL), lambda i: (mod_row(i), 0, 0)),
            const(norm_g), const(w_in_p), const(w_dtT), const(qg), const(kg),
            pl.BlockSpec((TM, QK_W), lambda i: (rope_row(i), 0)),
            pl.BlockSpec((TM, QK_W), lambda i: (rope_row(i), 0)),
            const(ones_bd),
        ],
        out_specs=[
            row(POOL_W), row(QK_W), pl.BlockSpec((QK_W, TM), lambda i: (0, i)), row(QK_W), row(ATT_W),
            row(SSM_W), row(CONV_CH), row(16), pl.BlockSpec((16, TM), lambda i: (0, i)),
        ],
        out_shape=outs,
        compiler_params=_cparams(("parallel",)),
        name="in_projection",
    )(x, mod_l, norm_g, w_in_p, w_dtT, qg, kg, cos_t, sin_t, ones_bd)


def _shift_kernel(seq_tiles_ref, pc_ref, pp_ref, pn_ref, xc_ref, xp_ref, xn_ref, wbd_ref, psc_ref, cw_ref, cb_ref,
                  pool_ref, xbc_ref):
    i = pl.program_id(0)
    st = seq_tiles_ref[i]
    pos = seq_tiles_ref[i + pl.num_programs(0)]
    has_prev = pos > 0
    has_next = pos < st - 1
    rows = TM + 2 * HALO
    r = lax.broadcasted_iota(jnp.int32, (rows, 1), 0)
    t = r - HALO + pos * TM
    t_len = st * TM

    def extended(c_ref, p_ref, n_ref):
        return jnp.concatenate([jnp.where(has_prev, p_ref[...], 0.0), c_ref[...],
                                jnp.where(has_next, n_ref[...], 0.0)], axis=0)

    def down(a, s):
        return pltpu.roll(a, s, 0)

    def up(a, s):
        return pltpu.roll(a, rows - s, 0)

    u = extended(pc_ref, pp_ref, pn_ref)
    w2 = u + down(u, 1)
    w4 = down(w2, 1) + up(w2, 1)
    w8 = down(w4, 2) + up(w4, 2)
    w16 = down(w8, 4) + up(w8, 4)
    lane = lax.broadcasted_iota(jnp.int32, (rows, POOL_W), 1)
    grp = lane // POOL_GROUP
    wsum = jnp.where(grp == 0, w2, jnp.where(grp == 1, w4, jnp.where(grp == 2, w8, w16)))
    half = jnp.where(grp == 0, 1, jnp.where(grp == 1, 2, jnp.where(grp == 2, 4, 8)))
    lo = jnp.maximum(t - half, 0)
    hi = jnp.minimum(t + half - 1, t_len - 1)
    cnt = (hi - lo + 1).astype(F32)
    pooled = (wsum / cnt - u)[HALO:HALO + TM]
    mixed = jnp.dot(pooled.astype(BF16), wbd_ref[...], preferred_element_type=F32)
    pool_ref[...] = mixed * psc_ref[...]

    x = extended(xc_ref, xp_ref, xn_ref)
    acc = x * cw_ref[2:3, :] + cb_ref[...]
    acc = acc + down(x, 2) * cw_ref[0:1, :] + down(x, 1) * cw_ref[1:2, :]
    acc = acc + up(x, 1) * cw_ref[3:4, :] + up(x, 2) * cw_ref[4:5, :]
    xbc_ref[...] = _silu(acc[HALO:HALO + TM])


def _token_shift(seq_info, pool_in, xbc, wbd, pool_scale, conv_w, conv_b):
    n = pool_in.shape[0]
    nt = n // TM
    hb = TM // HALO
    nhb = n // HALO

    cur = lambda w: pl.BlockSpec((TM, w), lambda i, s: (i, 0))
    prev = lambda w: pl.BlockSpec((HALO, w), lambda i, s: (jnp.maximum(i * hb - 1, 0), 0))
    nxt = lambda w: pl.BlockSpec((HALO, w), lambda i, s: (jnp.minimum((i + 1) * hb, nhb - 1), 0))
    const = lambda a: pl.BlockSpec(a.shape, lambda i, s: (0,) * a.ndim)
    grid_spec = pltpu.PrefetchScalarGridSpec(
        num_scalar_prefetch=1,
        grid=(nt,),
        in_specs=[cur(POOL_W), prev(POOL_W), nxt(POOL_W), cur(CONV_CH), prev(CONV_CH), nxt(CONV_CH),
                  const(wbd), const(pool_scale), const(conv_w), const(conv_b)],
        out_specs=[cur(POOL_W), cur(CONV_CH)],
    )
    return pl.pallas_call(
        _shift_kernel,
        grid_spec=grid_spec,
        out_shape=[jax.ShapeDtypeStruct((n, POOL_W), F32), jax.ShapeDtypeStruct((n, CONV_CH), F32)],
        compiler_params=_cparams(("parallel",)),
        name="token_shift",
    )(seq_info, pool_in, pool_in, pool_in, xbc, xbc, xbc, wbd, pool_scale, conv_w, conv_b)


def _attn_kernel(lam_ref, q_ref, kT_ref, v_ref, g_ref, o_ref, *, lam_init):
    lam = lam_ref[0]
    c0 = (ATT_QK_DIM ** -0.5) * math.log2(math.e)
    q = q_ref[...]
    outs = []
    for h in range(N_ATT_HEADS):
        probs = []
        for c in range(2):
            off = h * 2 * ATT_QK_DIM + c * ATT_QK_DIM
            s = jnp.dot(q[:, off:off + ATT_QK_DIM], kT_ref[0, off:off + ATT_QK_DIM, :], preferred_element_type=F32)
            m = jnp.max(s, axis=-1, keepdims=True)
            p = jnp.exp2((s - m) * c0)
            probs.append(p * (1.0 / jnp.sum(p, axis=-1, keepdims=True)))
        w = probs[0] - lam * probs[1]
        o = jnp.dot(w.astype(BF16), v_ref[0, h], preferred_element_type=F32)
        o = o * lax.rsqrt(jnp.mean(o * o, axis=-1, keepdims=True) + EPS) * g_ref[...] * (1.0 - lam_init)
        outs.append(o)
    o_ref[...] = jnp.concatenate(outs, axis=-1)


def _attention(lam, qn, kT, vh, subln_g, row0, n_seq, t_seq, prev_out, lam_init, n_total):
    s_all = kT.shape[-1]
    tq = TM
    qb = t_seq // tq
    b0 = row0 // tq
    kernel = functools.partial(_attn_kernel, lam_init=lam_init)
    in_specs = [
        pl.BlockSpec(memory_space=pltpu.SMEM),
        pl.BlockSpec((tq, QK_W), lambda b, i: (b0 + b * qb + i, 0)),
        pl.BlockSpec((1, QK_W, s_all), lambda b, i: (b, 0, 0)),
        pl.BlockSpec((1, N_ATT_HEADS, s_all, ATT_V_DIM), lambda b, i: (b, 0, 0, 0)),
        pl.BlockSpec((1, ATT_V_DIM), lambda b, i: (0, 0)),
    ]
    args = [lam, qn, kT, vh, subln_g]
    aliases = {}
    if prev_out is not None:
        in_specs.append(pl.BlockSpec(memory_space=pl.ANY))
        args.append(prev_out)
        aliases = {5: 0}
        kernel = functools.partial(_attn_kernel_aliased, lam_init=lam_init)
    return pl.pallas_call(
        kernel,
        grid=(n_seq, qb),
        in_specs=in_specs,
        out_specs=pl.BlockSpec((tq, ATT_W), lambda b, i: (b0 + b * qb + i, 0)),
        out_shape=jax.ShapeDtypeStruct((n_total, ATT_W), F32),
        input_output_aliases=aliases,
        compiler_params=_cparams(("parallel", "parallel")),
        name="diff_attention",
    )(*args)


def _attn_kernel_aliased(lam_ref, q_ref, kT_ref, v_ref, g_ref, prev_ref, o_ref, *, lam_init):
    del prev_ref
    _attn_kernel(lam_ref, q_ref, kT_ref, v_ref, g_ref, o_ref, lam_init=lam_init)


def _ssd_kernel(sched_ref, xbc_ref, dt_ref, dtT_ref, dtb_ref, dtbT_ref, ac_ref, acT_ref, dsk_ref, h0_ref,
                y_ref, hfin_ref, h_scr):
    s = pl.program_id(0)
    ns = pl.num_programs(0)
    fwd = sched_ref[ns + s] == 0
    first = sched_ref[2 * ns + s] == 1
    last = sched_ref[3 * ns + s] == 1
    Q = SSM_CHUNK

    @pl.when(first)
    def _():
        h_scr[...] = h0_ref[0, 0]

    xbc = xbc_ref[...]
    xs = xbc[:, :SSM_W]
    dt = _softplus(dt_ref[...] + dtb_ref[...])
    dtT = _softplus(dtT_ref[...] + dtbT_ref[...])
    a = dt * ac_ref[...]
    aT = dtT * acT_ref[...]
    li = lax.broadcasted_iota(jnp.int32, (Q, Q), 0)
    si = lax.broadcasted_iota(jnp.int32, (Q, Q), 1)
    valid = jnp.where(fwd, li - si, si - li) >= 0
    tri = valid.astype(F32)
    cs = jnp.dot(tri, a, preferred_element_type=F32, precision=HIGHEST)
    csT = lax.dot_general(aT, tri, (((1,), (1,)), ((), ())), preferred_element_type=F32, precision=HIGHEST)
    tot = jnp.sum(a, axis=0, keepdims=True)

    def pick(arr, h, axis):
        lo = lax.slice_in_dim(arr, h, h + 1, axis=axis)
        hi = lax.slice_in_dim(arr, N_SSM_HEADS + h, N_SSM_HEADS + h + 1, axis=axis)
        return jnp.where(fwd, lo, hi)

    ys = []
    per_group = N_SSM_HEADS // SSM_GROUPS
    for g in range(SSM_GROUPS):
        bg = xbc[:, SSM_W + g * SSM_STATE:SSM_W + (g + 1) * SSM_STATE]
        cg = xbc[:, SSM_W + SSM_GROUPS * SSM_STATE + g * SSM_STATE:SSM_W + SSM_GROUPS * SSM_STATE + (g + 1) * SSM_STATE]
        cgb = cg.astype(BF16)
        gram = lax.dot_general(cgb, bg.astype(BF16), (((1,), (1,)), ((), ())), preferred_element_type=F32)
        for hh in range(per_group):
            h = g * per_group + hh
            col = pick(cs, h, 1)
            rowv = pick(csT, h, 0)
            dth = pick(dt, h, 1)
            toth = pick(tot, h, 1)
            decay = jnp.exp(jnp.where(valid, col - rowv, -jnp.inf))
            xh = xs[:, h * SSM_HEAD_DIM:(h + 1) * SSM_HEAD_DIM]
            xd = (xh * dth).astype(BF16)
            y = jnp.dot((gram * decay).astype(BF16), xd, preferred_element_type=F32)
            hT = h_scr[h]
            y = y + jnp.dot(cgb, hT.astype(BF16), preferred_element_type=F32) * jnp.exp(col)
            bd = (bg * jnp.exp(toth - col)).astype(BF16)
            upd = lax.dot_general(bd, xd, (((0,), (0,)), ((), ())), preferred_element_type=F32)
            h_scr[h] = jnp.exp(toth) * hT + upd
            ys.append(y)
    y = jnp.concatenate(ys, axis=-1)
    y_ref[0] = y + jnp.where(fwd, 1.0, 0.0) * xs * dsk_ref[...]

    @pl.when(last)
    def _():
        hfin_ref[0, 0] = h_scr[...]


def _ssd(sched, xbc_act, dt, dtT, dt_bias, dt_biasT, a_coef, a_coefT, d_skip_l, h0T, n_seq):
    n = xbc_act.shape[0]
    ns = sched.shape[0] // 5
    Q = SSM_CHUNK
    const = lambda a: pl.BlockSpec(a.shape, lambda s, sc: (0,) * a.ndim)
    state_spec = pl.BlockSpec((1, 1, N_SSM_HEADS, SSM_STATE, SSM_HEAD_DIM),
                              lambda s, sc: (sc[4 * ns + s], sc[ns + s], 0, 0, 0))
    grid_spec = pltpu.PrefetchScalarGridSpec(
        num_scalar_prefetch=1,
        grid=(ns,),
        in_specs=[
            pl.BlockSpec((Q, CONV_CH), lambda s, sc: (sc[s], 0)),
            pl.BlockSpec((Q, 16), lambda s, sc: (sc[s], 0)),
            pl.BlockSpec((16, Q), lambda s, sc: (0, sc[s])),
            const(dt_bias), const(dt_biasT), const(a_coef), const(a_coefT), const(d_skip_l),
            state_spec,
        ],
        out_specs=[
            pl.BlockSpec((1, Q, SSM_W), lambda s, sc: (sc[ns + s], sc[s], 0)),
            state_spec,
        ],
        scratch_shapes=[pltpu.VMEM((N_SSM_HEADS, SSM_STATE, SSM_HEAD_DIM), F32)],
    )
    return pl.pallas_call(
        _ssd_kernel,
        grid_spec=grid_spec,
        out_shape=[
            jax.ShapeDtypeStruct((N_DIRS, n, SSM_W), F32),
            jax.ShapeDtypeStruct((n_seq, N_DIRS, N_SSM_HEADS, SSM_STATE, SSM_HEAD_DIM), F32),
        ],
        compiler_params=_cparams(("arbitrary",)),
        name="ssd_scan",
    )(sched, xbc_act, dt, dtT, dt_bias, dt_biasT, a_coef, a_coefT, d_skip_l, h0T)


def _outproj_kernel(x_ref, pool_ref, att_ref, y_ref, z_ref, sg_ref, w_ref, mod_ref, g2_ref, wr_ref,
                    x1_ref, h2_ref, sc_ref):
    y = (y_ref[0] + y_ref[1]) * _silu(z_ref[...])
    half = SSM_W // SSM_GROUPS
    parts = []
    for g in range(SSM_GROUPS):
        yg = y[:, g * half:(g + 1) * half]
        parts.append(yg * lax.rsqrt(jnp.mean(yg * yg, axis=-1, keepdims=True) + EPS))
    ssm = jnp.concatenate(parts, axis=-1) * sg_ref[...]
    cat = jnp.concatenate([pool_ref[...], att_ref[...], ssm], axis=-1).astype(BF16)
    mixed = jnp.dot(cat, w_ref[...], preferred_element_type=F32)
    x1 = x_ref[...] + mod_ref[0, 2:3, :] * mixed
    x1_ref[...] = x1
    h2 = x1 * lax.rsqrt(jnp.mean(x1 * x1, axis=-1, keepdims=True) + EPS) * g2_ref[...]
    h2 = h2 * (1.0 + mod_ref[0, 4:5, :]) + mod_ref[0, 3:4, :]
    h2_ref[...] = h2
    logits = lax.dot_general(wr_ref[...], h2, (((1,), (1,)), ((), ())), preferred_element_type=F32, precision=HIGHEST)
    sc_ref[...] = jax.nn.sigmoid(logits)


def _out_projection(x, pool_out, att, y2, z, ssm_g, w_out, mod_l, norm2_g, w_rT, n_ctx_tiles, lat_tiles):
    n = x.shape[0]
    nt = n // TM

    def mod_row(i):
        return jnp.where(i < n_ctx_tiles, 0, 1 + (i - n_ctx_tiles) // lat_tiles)

    row = lambda w: pl.BlockSpec((TM, w), lambda i: (i, 0))
    const = lambda a: pl.BlockSpec(a.shape, lambda i: (0,) * a.ndim)
    return pl.pallas_call(
        _outproj_kernel,
        grid=(nt,),
        in_specs=[
            row(D_MODEL), row(POOL_W), row(ATT_W),
            pl.BlockSpec((N_DIRS, TM, SSM_W), lambda i: (0, i, 0)),
            row(SSM_W), const(ssm_g), const(w_out),
            pl.BlockSpec((1, 6, D_MODEL), lambda i: (mod_row(i), 0, 0)),
            const(norm2_g), const(w_rT),
        ],
        out_specs=[row(D_MODEL), row(D_MODEL), pl.BlockSpec((N_EXPERTS, TM), lambda i: (0, i))],
        out_shape=[
            jax.ShapeDtypeStruct((n, D_MODEL), F32),
            jax.ShapeDtypeStruct((n, D_MODEL), F32),
            jax.ShapeDtypeStruct((N_EXPERTS, n), F32),
        ],
        compiler_params=_cparams(("parallel",)),
        name="out_projection",
    )(x, pool_out, att, y2, z, ssm_g, w_out, mod_l, norm2_g, w_rT)


def _moe_kernel(meta_ref, gidx_ref, sidx_ref, wrow_ref, x_ref, wg_ref, wu_ref, wd_ref, acc_ref, xg_scr, y_scr):
    b = pl.program_id(0)
    nb = pl.num_programs(0)
    valid = meta_ref[2 * nb + b] == 1
    first = meta_ref[3 * nb + b] == 1
    R = MOE_BLK

    @pl.when(first)
    def _():
        acc_ref[...] = jnp.zeros_like(acc_ref)

    @pl.when(valid)
    def _():
        for r in range(R):
            src = pl.multiple_of(gidx_ref[0, 0, r] * SUBLANES, SUBLANES)
            xg_scr[r * SUBLANES:(r + 1) * SUBLANES, :] = x_ref[0, pl.ds(src, SUBLANES), :]
        chunks = [xg_scr[pl.ds(j, R, stride=SUBLANES), :] for j in range(D_MODEL // LANES)]
        xb = jnp.concatenate(chunks, axis=-1).astype(BF16)
        hg = jnp.dot(xb, wg_ref[0], preferred_element_type=F32)
        hu = jnp.dot(xb, wu_ref[0], preferred_element_type=F32)
        hm = (_silu(hg) * hu).astype(BF16)
        y = jnp.dot(hm, wd_ref[0], preferred_element_type=F32)
        for j in range(D_MODEL // LANES):
            y_scr[pl.ds(j, R, stride=SUBLANES), :] = y[:, j * LANES:(j + 1) * LANES]
        group = SUBLANES
        for g0 in range(0, R, group):
            pend = []
            for r in range(g0, g0 + group):
                dst = pl.multiple_of(sidx_ref[0, 0, r] * SUBLANES, SUBLANES)
                cur = acc_ref[0, pl.ds(dst, SUBLANES), :]
                pend.append((dst, cur + y_scr[r * SUBLANES:(r + 1) * SUBLANES, :] * wrow_ref[0, 0, r]))
            for dst, val in pend:
                acc_ref[0, pl.ds(dst, SUBLANES), :] = val


def _routed_experts(meta, gidx, sidx, wrow, x_rows, w_gate, w_up, w_down, n_tiles):
    nb = gidx.shape[0]
    R = MOE_BLK
    rows_in = MOE_TS * SUBLANES
    rows_out = (MOE_TS + SUBLANES) * SUBLANES
    smem_blk = lambda: pl.BlockSpec((1, 1, R), lambda b, m: (b, 0, 0), memory_space=pltpu.SMEM)
    grid_spec = pltpu.PrefetchScalarGridSpec(
        num_scalar_prefetch=1,
        grid=(nb,),
        in_specs=[
            smem_blk(), smem_blk(), smem_blk(),
            pl.BlockSpec((1, rows_in, LANES), lambda b, m: (m[b], 0, 0)),
            pl.BlockSpec((1, D_MODEL, EXPERT_FF), lambda b, m: (m[nb + b], 0, 0)),
            pl.BlockSpec((1, D_MODEL, EXPERT_FF), lambda b, m: (m[nb + b], 0, 0)),
            pl.BlockSpec((1, EXPERT_FF, D_MODEL), lambda b, m: (m[nb + b], 0, 0)),
        ],
        out_specs=pl.BlockSpec((1, rows_out, LANES), lambda b, m: (m[b], 0, 0)),
        scratch_shapes=[pltpu.VMEM((R * SUBLANES, LANES), F32), pltpu.VMEM((R * SUBLANES, LANES), F32)],
    )
    return pl.pallas_call(
        _moe_kernel,
        grid_spec=grid_spec,
        out_shape=jax.ShapeDtypeStruct((n_tiles, rows_out, LANES), F32),
        compiler_params=_cparams(("arbitrary",)),
        name="routed_experts",
    )(meta, gidx, sidx, wrow, x_rows, w_gate, w_up, w_down)


def _route(scores, router_bias):
    n = scores.shape[0]
    per_group = N_EXPERTS // N_EXPERT_GROUPS
    sel = scores + router_bias
    group_score = lax.top_k(sel.reshape(n, N_EXPERT_GROUPS, per_group), 2)[0].sum(-1)
    _, gidx = lax.top_k(group_score, TOPK_GROUPS)
    gmask = jax.nn.one_hot(gidx, N_EXPERT_GROUPS, dtype=F32).sum(1) > 0
    sel = jnp.where(jnp.repeat(gmask, per_group, axis=1), sel, -jnp.inf)
    _, eidx = lax.top_k(sel, TOP_K)
    wts = jnp.take_along_axis(scores, eidx, axis=1)
    wts = wts / jnp.sum(wts, axis=-1, keepdims=True) * ROUTED_SCALE
    return eidx, wts


def _dispatch_plan(eidx, wts):
    n = eidx.shape[0]
    n_tiles = n // MOE_TS
    s = n * TOP_K
    nkeys = n_tiles * N_EXPERTS
    flat_e = eidx.reshape(-1).astype(jnp.int32)
    flat_tok = jnp.arange(s, dtype=jnp.int32) // TOP_K
    key = (flat_tok // MOE_TS) * N_EXPERTS + flat_e
    order = jnp.argsort(key)
    skey = key[order]
    stok = flat_tok[order] % MOE_TS
    sw = wts.reshape(-1)[order]
    counts = jnp.bincount(key, length=nkeys)
    padded = (counts + MOE_BLK - 1) // MOE_BLK * MOE_BLK
    pend = jnp.cumsum(padded)
    pstart = pend - padded
    ustart = jnp.cumsum(counts) - counts
    dest = pstart[skey] + jnp.arange(s, dtype=jnp.int32) - ustart[skey]
    nb = s // MOE_BLK + nkeys
    rows = nb * MOE_BLK
    gidx = jnp.zeros((rows,), jnp.int32).at[dest].set(stok)
    sidx = (MOE_TS + jnp.arange(rows, dtype=jnp.int32) % SUBLANES).at[dest].set(stok)
    wrow = jnp.zeros((rows,), F32).at[dest].set(sw)
    used = pend[-1] // MOE_BLK
    blk = jnp.arange(nb, dtype=jnp.int32)
    valid = blk < used
    bkey = jnp.minimum(jnp.searchsorted(pend, blk * MOE_BLK, side="right"), nkeys - 1).astype(jnp.int32)
    last_key = bkey[jnp.maximum(used - 1, 0)]
    bkey = jnp.where(valid, bkey, last_key)
    btile = bkey // N_EXPERTS
    bexp = bkey % N_EXPERTS
    first = valid & jnp.concatenate([jnp.ones((1,), bool), btile[1:] != btile[:-1]])
    meta = jnp.concatenate([btile, bexp, valid.astype(jnp.int32), first.astype(jnp.int32)]).astype(jnp.int32)
    shp = (nb, 1, MOE_BLK)
    return meta, gidx.reshape(shp), sidx.reshape(shp), wrow.reshape(shp)


def _ffn_out_kernel(x1_ref, h2_ref, r_ref, wg_ref, wu_ref, wd_ref, mod_ref, o_ref):
    hb = h2_ref[...].astype(BF16)
    hg = jnp.dot(hb, wg_ref[...], preferred_element_type=F32)
    hu = jnp.dot(hb, wu_ref[...], preferred_element_type=F32)
    shared = jnp.dot((_silu(hg) * hu).astype(BF16), wd_ref[...], preferred_element_type=F32)
    o_ref[...] = x1_ref[...] + mod_ref[0, 5:6, :] * (r_ref[...] + shared)


def _ffn_out(x1, h2, routed, ws_gate, ws_up, ws_down, mod_l, n_ctx_tiles, lat_tiles):
    n = x1.shape[0]
    nt = n // TM

    def mod_row(i):
        return jnp.where(i < n_ctx_tiles, 0, 1 + (i - n_ctx_tiles) // lat_tiles)

    row = lambda w: pl.BlockSpec((TM, w), lambda i: (i, 0))
    const = lambda a: pl.BlockSpec(a.shape, lambda i: (0,) * a.ndim)
    return pl.pallas_call(
        _ffn_out_kernel,
        grid=(nt,),
        in_specs=[row(D_MODEL), row(D_MODEL), row(D_MODEL), const(ws_gate), const(ws_up), const(ws_down),
                  pl.BlockSpec((1, 6, D_MODEL), lambda i: (mod_row(i), 0, 0))],
        out_specs=row(D_MODEL),
        out_shape=jax.ShapeDtypeStruct((n, D_MODEL), F32),
        compiler_params=_cparams(("parallel",)),
        name="ffn_out",
    )(x1, h2, routed, ws_gate, ws_up, ws_down, mod_l)


def _rope_tables(t_seq):
    rows = t_seq // GRID_W
    r, col = jnp.meshgrid(jnp.arange(rows), jnp.arange(GRID_W), indexing="ij")
    pos = jnp.stack([r.reshape(-1), col.reshape(-1)], axis=-1).astype(F32)
    inv = ROPE_THETA ** (-jnp.arange(ROPE_FREQS, dtype=F32) / ROPE_FREQS)
    ang = pos[:, :, None] * inv
    cos = jnp.cos(ang).reshape(t_seq, 2 * ROPE_FREQS)
    sin = jnp.sin(ang).reshape(t_seq, 2 * ROPE_FREQS)
    cos32 = jnp.concatenate([cos, cos], axis=-1)
    sin32 = jnp.concatenate([-sin, sin], axis=-1)
    reps = QK_W // ATT_QK_DIM
    cos_t = jnp.tile(cos32, (1, reps))
    sin_t = jnp.tile(sin32, (1, reps))
    cos_t = jnp.concatenate([cos_t, jnp.ones((TM, QK_W), F32)], axis=0)
    sin_t = jnp.concatenate([sin_t, jnp.zeros((TM, QK_W), F32)], axis=0)
    return cos_t, sin_t


def _block_diag(blocks):
    g, a, b = blocks.shape
    out = jnp.zeros((g * a, g * b), blocks.dtype)
    for i in range(g):
        out = out.at[i * a:(i + 1) * a, i * b:(i + 1) * b].set(blocks[i])
    return out


def _ssd_schedule(n_ctx_seq, ctx_chunks, n_lat_seq, lat_chunks):
    blk, dirs, first, last, seq = [], [], [], [], []
    base = 0
    sid = 0
    for n_seq, nc in ((n_ctx_seq, ctx_chunks), (n_lat_seq, lat_chunks)):
        for _ in range(n_seq):
            for d in range(N_DIRS):
                order = range(nc) if d == 0 else range(nc - 1, -1, -1)
                for k, c in enumerate(order):
                    blk.append(base + c)
                    dirs.append(d)
                    first.append(int(k == 0))
                    last.append(int(k == nc - 1))
                    seq.append(sid)
            base += nc
            sid += 1
    return jnp.asarray(blk + dirs + first + last + seq, jnp.int32)


def kernel(x_prompt, x_sample, cache_k, cache_v, state_ssm, c, c_ctx, w_mod, b_mod, norm1_g, w_in, pool_w, pool_scale, q_norm_g, k_norm_g, lambda_q, lambda_k, subln_g, conv_w, conv_b, dt_bias, a_log, d_skip, ssm_norm_g, w_out, norm2_g, w_router, router_bias, w_gate, w_up, w_down, ws_gate, ws_up, ws_down):
    nb_ctx, t_ctx, d = x_prompt.shape
    nb_lat, t_lat, _ = x_sample.shape
    depth = w_mod.shape[0]
    past = cache_k.shape[2]
    n_ctx = nb_ctx * t_ctx
    n_lat = nb_lat * t_lat
    n = n_ctx + n_lat
    assert d == D_MODEL and t_ctx % TM == 0 and t_lat % TM == 0 and n % MOE_TS == 0
    assert 1 + nb_lat <= SUBLANES
    n_ctx_tiles = n_ctx // TM
    lat_tiles = t_lat // TM
    n_seq = nb_ctx + nb_lat

    x = jnp.concatenate([x_prompt.reshape(n_ctx, d), x_sample.reshape(n_lat, d)], axis=0)
    cond = jnp.zeros((SUBLANES, d), F32).at[0].set(c_ctx).at[1:1 + nb_lat].set(c)
    mod = _modulation(cond, w_mod, b_mod).reshape(depth, SUBLANES, 6, d)

    cos_t, sin_t = _rope_tables(t_lat)
    ones_bd = _block_diag(jnp.ones((QK_W // ATT_QK_DIM, ATT_QK_DIM, ATT_QK_DIM), F32))
    seq_tiles = jnp.asarray([t_ctx // TM] * n_ctx_tiles + [lat_tiles] * (nb_lat * lat_tiles), jnp.int32)
    tile_pos = jnp.asarray([i % (t_ctx // TM) for i in range(n_ctx_tiles)] + [i % lat_tiles for i in range(nb_lat * lat_tiles)], jnp.int32)
    seq_info = jnp.concatenate([seq_tiles, tile_pos])
    sched = _ssd_schedule(nb_ctx, t_ctx // SSM_CHUNK, nb_lat, t_lat // SSM_CHUNK)

    new_k, new_v, new_ssm = [], [], []
    for l in range(depth):
        mod_l = mod[l]
        w_in_p = jnp.pad(w_in[l], ((0, 0), (0, IN_W_PAD - IN_W))).astype(BF16)
        w_dtT = w_in[l][:, IN_W - 16:].T.astype(BF16)
        qg = jnp.tile(q_norm_g[l], QK_W // ATT_QK_DIM)[None]
        kg = jnp.tile(k_norm_g[l], QK_W // ATT_QK_DIM)[None]
        pool_in, qn, kT, kn, v, z, xbc, dt, dtT = _in_projection(
            x, mod_l, norm1_g[l][None], w_in_p, w_dtT, qg, kg, cos_t, sin_t, ones_bd, n_ctx_tiles, lat_tiles)

        wbd = _block_diag(pool_w[l]).astype(BF16)
        pool_out, xbc_act = _token_shift(seq_info, pool_in, xbc, wbd, pool_scale[l][None], conv_w[l], conv_b[l][None])

        lam_init = 0.8 - 0.6 * math.exp(-0.3 * l)
        lq = lambda_q[l].astype(F32)
        lk = lambda_k[l].astype(F32)
        lam = (jnp.exp(jnp.sum(lq[0] * lk[0])) - jnp.exp(jnp.sum(lq[1] * lk[1])) + lam_init).reshape(1)
        vb = v.astype(BF16)
        kT_ctx = kT[:, :n_ctx].reshape(QK_W, nb_ctx, t_ctx).transpose(1, 0, 2)
        v_ctx = vb[:n_ctx].reshape(nb_ctx, t_ctx, N_ATT_HEADS, ATT_V_DIM).transpose(0, 2, 1, 3)
        kT_lat = kT[:, n_ctx:].reshape(QK_W, nb_lat, t_lat).transpose(1, 0, 2)
        ck = cache_k[:, l].reshape(nb_lat, past, QK_W).transpose(0, 2, 1).astype(BF16)
        kT_lat = jnp.concatenate([kT_lat, ck], axis=2)
        v_lat = vb[n_ctx:].reshape(nb_lat, t_lat, N_ATT_HEADS, ATT_V_DIM).transpose(0, 2, 1, 3)
        cv = cache_v[:, l].astype(BF16).transpose(0, 2, 1, 3)
        v_lat = jnp.concatenate([v_lat, cv], axis=2)
        sg = subln_g[l][None]
        att = _attention(lam, qn, kT_ctx, v_ctx, sg, 0, nb_ctx, t_ctx, None, lam_init, n)
        att = _attention(lam, qn, kT_lat, v_lat, sg, n_ctx, nb_lat, t_lat, att, lam_init, n)

        h0T = jnp.concatenate([
            jnp.zeros((nb_ctx, N_DIRS, N_SSM_HEADS, SSM_STATE, SSM_HEAD_DIM), F32),
            state_ssm[:, l].astype(F32).transpose(0, 1, 2, 4, 3)], axis=0)
        dtb = dt_bias[l].reshape(1, N_DIRS * N_SSM_HEADS)
        ac = (-jnp.exp(a_log[l].astype(F32))).reshape(1, N_DIRS * N_SSM_HEADS)
        dsk = jnp.repeat(d_skip[l].astype(F32), SSM_HEAD_DIM)[None]
        y2, hfinT = _ssd(sched, xbc_act, dt, dtT, dtb, dtb.T, ac, ac.T, dsk, h0T, n_seq)

        x1, h2, scoresT = _out_projection(
            x, pool_out, att, y2, z, ssm_norm_g[l][None], w_out[l].astype(BF16), mod_l, norm2_g[l][None],
            w_router[l].T, n_ctx_tiles, lat_tiles)

        eidx, wts = _route(scoresT.T, router_bias[l].astype(F32))
        meta, gidx, sidx, wrow = _dispatch_plan(eidx, wts)
        n_tiles = n // MOE_TS
        x_rows = h2.reshape(n_tiles, MOE_TS * SUBLANES, LANES)
        acc = _routed_experts(meta, gidx, sidx, wrow, x_rows, w_gate[l].astype(BF16), w_up[l].astype(BF16),
                              w_down[l].astype(BF16), n_tiles)
        routed = acc[:, :MOE_TS * SUBLANES].reshape(n, d)
        x = _ffn_out(x1, h2, routed, ws_gate[l].astype(BF16), ws_up[l].astype(BF16), ws_down[l].astype(BF16),
                     mod_l, n_ctx_tiles, lat_tiles)

        new_k.append(kn[:n_ctx].reshape(nb_ctx, t_ctx, N_ATT_HEADS, 2, ATT_QK_DIM))
        new_v.append(v[:n_ctx].reshape(nb_ctx, t_ctx, N_ATT_HEADS, ATT_V_DIM))
        new_ssm.append(hfinT[:nb_ctx].transpose(0, 1, 2, 4, 3))

    y_prompt = x[:n_ctx].reshape(nb_ctx, t_ctx, d)
    y_sample = x[n_ctx:].reshape(nb_lat, t_lat, d)
    return (y_prompt, y_sample, jnp.stack(new_k, axis=1), jnp.stack(new_v, axis=1), jnp.stack(new_ssm, axis=1))
```

```python
import functools
import math

import jax
import jax.numpy as jnp
from jax import lax
from jax.experimental import pallas as pl
from jax.experimental.pallas import tpu as pltpu

F32 = jnp.float32
BF16 = jnp.bfloat16

D_MODEL = 1024
GRID_W = 64
POOL_W = 256
POOL_WINDOWS = (2, 4, 8, 16)
POOL_GROUP = 64
N_ATT_HEADS = 4
ATT_V_DIM = 64
ATT_QK_DIM = 32
ATT_W = 256
QK_W = 256
ROPE_FREQS = 8
ROPE_THETA = 10000.0
SSM_W = 512
SSM_HEAD_DIM = 64
N_SSM_HEADS = 8
SSM_GROUPS = 2
SSM_STATE = 64
SSM_CONV = 5
SSM_CHUNK = 128
N_DIRS = 2
CONV_CH = 768
N_EXPERTS = 64
N_EXPERT_GROUPS = 8
TOPK_GROUPS = 4
TOP_K = 8
EXPERT_FF = 256
ROUTED_SCALE = 2.5
EPS = 1e-6
IN_W = 2320
IN_W_PAD = 2432

LANES = 128
SUBLANES = 8
TM = 256
HALO = 8
MOE_TS = 2048
MOE_BLK = 128
VMEM_LIMIT = 48 * 1024 * 1024

HIGHEST = lax.Precision.HIGHEST


def _cparams(sem):
    return pltpu.CompilerParams(dimension_semantics=sem, vmem_limit_bytes=VMEM_LIMIT)


def _silu(x):
    return x * jax.nn.sigmoid(x)


def _softplus(x):
    return jnp.maximum(x, 0.0) + jnp.log1p(jnp.exp(-jnp.abs(x)))


def _mod_kernel(c_ref, w_ref, b_ref, o_ref):
    c = c_ref[...]
    o_ref[0] = jnp.dot(_silu(c), w_ref[0], preferred_element_type=F32, precision=HIGHEST) + b_ref[0]


def _modulation(cond, w_mod, b_mod):
    L = w_mod.shape[0]
    bn = 1536
    return pl.pallas_call(
        _mod_kernel,
        grid=(L, 6 * D_MODEL // bn),
        in_specs=[
            pl.BlockSpec((SUBLANES, D_MODEL), lambda l, j: (0, 0)),
            pl.BlockSpec((1, D_MODEL, bn), lambda l, j: (l, 0, j)),
            pl.BlockSpec((1, 1, bn), lambda l, j: (l, 0, j)),
        ],
        out_specs=pl.BlockSpec((1, SUBLANES, bn), lambda l, j: (l, 0, j)),
        out_shape=jax.ShapeDtypeStruct((L, SUBLANES, 6 * D_MODEL), F32),
        compiler_params=_cparams(("parallel", "parallel")),
        name="adaln_mod",
    )(cond, w_mod, b_mod.reshape(L, 1, 6 * D_MODEL))


def _inproj_kernel(x_ref, mod_ref, g_ref, w_ref, wdt_ref, qg_ref, kg_ref, cos_ref, sin_ref, ones_ref,
                   pool_ref, qn_ref, kT_ref, kn_ref, v_ref, z_ref, xbc_ref, dt_ref, dtT_ref):
    x = x_ref[...]
    xn = x * lax.rsqrt(jnp.mean(x * x, axis=-1, keepdims=True) + EPS) * g_ref[...]
    h = xn * (1.0 + mod_ref[0, 1:2, :]) + mod_ref[0, 0:1, :]
    hb = h.astype(BF16)
    u = jnp.dot(hb, w_ref[...], preferred_element_type=F32)
    pool_ref[...] = u[:, 0:256]
    v_ref[...] = u[:, 768:1024]
    z_ref[...] = u[:, 1024:1536]
    xbc_ref[...] = u[:, 1536:2304]
    dt_ref[...] = u[:, 2304:2304 + LANES][:, 0:16]
    dtT_ref[...] = lax.dot_general(wdt_ref[...], hb, (((1,), (1,)), ((), ())), preferred_element_type=F32)

    lane = lax.broadcasted_iota(jnp.int32, (TM, QK_W), 1)
    first_half = (lane % ATT_QK_DIM) < (ATT_QK_DIM // 2)
    cos = cos_ref[...]
    sin = sin_ref[...]

    def qk_norm(t, gain):
        ss = jnp.dot(t * t, ones_ref[...], preferred_element_type=F32, precision=HIGHEST)
        return t * lax.rsqrt(ss * (1.0 / ATT_QK_DIM) + EPS) * gain

    def rope(t):
        partner = jnp.where(first_half, pltpu.roll(t, QK_W - ATT_QK_DIM // 2, 1), pltpu.roll(t, ATT_QK_DIM // 2, 1))
        return t * cos + partner * sin

    qn_ref[...] = rope(qk_norm(u[:, 256:512], qg_ref[...])).astype(BF16)
    kn = qk_norm(u[:, 512:768], kg_ref[...])
    kn_ref[...] = kn
    kT_ref[...] = rope(kn).T.astype(BF16)


def _in_projection(x, mod_l, norm_g, w_in_p, w_dtT, qg, kg, cos_t, sin_t, ones_bd, n_ctx_tiles, lat_tiles):
    n = x.shape[0]
    nt = n // TM

    def mod_row(i):
        return jnp.where(i < n_ctx_tiles, 0, 1 + (i - n_ctx_tiles) // lat_tiles)

    def rope_row(i):
        return jnp.where(i < n_ctx_tiles, lat_tiles, (i - n_ctx_tiles) % lat_tiles)

    row = lambda w: pl.BlockSpec((TM, w), lambda i: (i, 0))
    const = lambda a: pl.BlockSpec(a.shape, lambda i: (0,) * a.ndim)
    outs = [
        jax.ShapeDtypeStruct((n, POOL_W), F32),
        jax.ShapeDtypeStruct((n, QK_W), BF16),
        jax.ShapeDtypeStruct((QK_W, n), BF16),
        jax.ShapeDtypeStruct((n, QK_W), F32),
        jax.ShapeDtypeStruct((n, ATT_W), F32),
        jax.ShapeDtypeStruct((n, SSM_W), F32),
        jax.ShapeDtypeStruct((n, CONV_CH), F32),
        jax.ShapeDtypeStruct((n, 16), F32),
        jax.ShapeDtypeStruct((16, n), F32),
    ]
    return pl.pallas_call(
        _inproj_kernel,
        grid=(nt,),
        in_specs=[
            row(D_MODEL),
            pl.BlockSpec((1, 6, D_MODEL), lambda i: (mod_row(i), 0, 0)),
            const(norm_g), const(w_in_p), const(w_dtT), const(qg), const(kg),
            pl.BlockSpec((TM, QK_W), lambda i: (rope_row(i), 0)),
            pl.BlockSpec((TM, QK_W), lambda i: (rope_row(i), 0)),
            const(ones_bd),
        ],
        out_specs=[
            row(POOL_W), row(QK_W), pl.BlockSpec((QK_W, TM), lambda i: (0, i)), row(QK_W), row(ATT_W),
            row(SSM_W), row(CONV_CH), row(16), pl.BlockSpec((16, TM), lambda i: (0, i)),
        ],
        out_shape=outs,
        compiler_params=_cparams(("parallel",)),
        name="in_projection",
    )(x, mod_l, norm_g, w_in_p, w_dtT, qg, kg, cos_t, sin_t, ones_bd)


def _shift_kernel(seq_tiles_ref, pc_ref, pp_ref, pn_ref, xc_ref, xp_ref, xn_ref, wbd_ref, psc_ref, cw_ref, cb_ref,
                  pool_ref, xbc_ref):
    i = pl.program_id(0)
    st = seq_tiles_ref[i]
    pos = seq_tiles_ref[i + pl.num_programs(0)]
    has_prev = pos > 0
    has_next = pos < st - 1
    rows = TM + 2 * HALO
    r = lax.broadcasted_iota(jnp.int32, (rows, 1), 0)
    t = r - HALO + pos * TM
    t_len = st * TM

    def extended(c_ref, p_ref, n_ref):
        return jnp.concatenate([jnp.where(has_prev, p_ref[...], 0.0), c_ref[...],
                                jnp.where(has_next, n_ref[...], 0.0)], axis=0)

    def down(a, s):
        return pltpu.roll(a, s, 0)

    def up(a, s):
        return pltpu.roll(a, rows - s, 0)

    u = extended(pc_ref, pp_ref, pn_ref)
    w2 = u + down(u, 1)
    w4 = down(w2, 1) + up(w2, 1)
    w8 = down(w4, 2) + up(w4, 2)
    w16 = down(w8, 4) + up(w8, 4)
    lane = lax.broadcasted_iota(jnp.int32, (rows, POOL_W), 1)
    grp = lane // POOL_GROUP
    wsum = jnp.where(grp == 0, w2, jnp.where(grp == 1, w4, jnp.where(grp == 2, w8, w16)))
    half = jnp.where(grp == 0, 1, jnp.where(grp == 1, 2, jnp.where(grp == 2, 4, 8)))
    lo = jnp.maximum(t - half, 0)
    hi = jnp.minimum(t + half - 1, t_len - 1)
    cnt = (hi - lo + 1).astype(F32)
    pooled = (wsum / cnt - u)[HALO:HALO + TM]
    mixed = jnp.dot(pooled.astype(BF16), wbd_ref[...], preferred_element_type=F32)
    pool_ref[...] = mixed * psc_ref[...]

    x = extended(xc_ref, xp_ref, xn_ref)
    acc = x * cw_ref[2:3, :] + cb_ref[...]
    acc = acc + down(x, 2) * cw_ref[0:1, :] + down(x, 1) * cw_ref[1:2, :]
    acc = acc + up(x, 1) * cw_ref[3:4, :] + up(x, 2) * cw_ref[4:5, :]
    xbc_ref[...] = _silu(acc[HALO:HALO + TM])


def _token_shift(seq_info, pool_in, xbc, wbd, pool_scale, conv_w, conv_b):
    n = pool_in.shape[0]
    nt = n // TM
    hb = TM // HALO
    nhb = n // HALO

    cur = lambda w: pl.BlockSpec((TM, w), lambda i, s: (i, 0))
    prev = lambda w: pl.BlockSpec((HALO, w), lambda i, s: (jnp.maximum(i * hb - 1, 0), 0))
    nxt = lambda w: pl.BlockSpec((HALO, w), lambda i, s: (jnp.minimum((i + 1) * hb, nhb - 1), 0))
    const = lambda a: pl.BlockSpec(a.shape, lambda i, s: (0,) * a.ndim)
    grid_spec = pltpu.PrefetchScalarGridSpec(
        num_scalar_prefetch=1,
        grid=(nt,),
        in_specs=[cur(POOL_W), prev(POOL_W), nxt(POOL_W), cur(CONV_CH), prev(CONV_CH), nxt(CONV_CH),
                  const(wbd), const(pool_scale), const(conv_w), const(conv_b)],
        out_specs=[cur(POOL_W), cur(CONV_CH)],
    )
    return pl.pallas_call(
        _shift_kernel,
        grid_spec=grid_spec,
        out_shape=[jax.ShapeDtypeStruct((n, POOL_W), F32), jax.ShapeDtypeStruct((n, CONV_CH), F32)],
        compiler_params=_cparams(("parallel",)),
        name="token_shift",
    )(seq_info, pool_in, pool_in, pool_in, xbc, xbc, xbc, wbd, pool_scale, conv_w, conv_b)


def _attn_kernel(lam_ref, q_ref, kT_ref, v_ref, g_ref, o_ref, *, lam_init):
    lam = lam_ref[0]
    c0 = (ATT_QK_DIM ** -0.5) * math.log2(math.e)
    q = q_ref[...]
    outs = []
    for h in range(N_ATT_HEADS):
        probs = []
        for c in range(2):
            off = h * 2 * ATT_QK_DIM + c * ATT_QK_DIM
            s = jnp.dot(q[:, off:off + ATT_QK_DIM], kT_ref[0, off:off + ATT_QK_DIM, :], preferred_element_type=F32)
            m = jnp.max(s, axis=-1, keepdims=True)
            p = jnp.exp2((s - m) * c0)
            probs.append(p * (1.0 / jnp.sum(p, axis=-1, keepdims=True)))
        w = probs[0] - lam * probs[1]
        o = jnp.dot(w.astype(BF16), v_ref[0, h], preferred_element_type=F32)
        o = o * lax.rsqrt(jnp.mean(o * o, axis=-1, keepdims=True) + EPS) * g_ref[...] * (1.0 - lam_init)
        outs.append(o)
    o_ref[...] = jnp.concatenate(outs, axis=-1)


def _attention(lam, qn, kT, vh, subln_g, row0, n_seq, t_seq, prev_out, lam_init, n_total):
    s_all = kT.shape[-1]
    tq = TM
    qb = t_seq // tq
    b0 = row0 // tq
    kernel = functools.partial(_attn_kernel, lam_init=lam_init)
    in_specs = [
        pl.BlockSpec(memory_space=pltpu.SMEM),
        pl.BlockSpec((tq, QK_W), lambda b, i: (b0 + b * qb + i, 0)),
        pl.BlockSpec((1, QK_W, s_all), lambda b, i: (b, 0, 0)),
        pl.BlockSpec((1, N_ATT_HEADS, s_all, ATT_V_DIM), lambda b, i: (b, 0, 0, 0)),
        pl.BlockSpec((1, ATT_V_DIM), lambda b, i: (0, 0)),
    ]
    args = [lam, qn, kT, vh, subln_g]
    aliases = {}
    if prev_out is not None:
        in_specs.append(pl.BlockSpec(memory_space=pl.ANY))
        args.append(prev_out)
        aliases = {5: 0}
        kernel = functools.partial(_attn_kernel_aliased, lam_init=lam_init)
    return pl.pallas_call(
        kernel,
        grid=(n_seq, qb),
        in_specs=in_specs,
        out_specs=pl.BlockSpec((tq, ATT_W), lambda b, i: (b0 + b * qb + i, 0)),
        out_shape=jax.ShapeDtypeStruct((n_total, ATT_W), F32),
        input_output_aliases=aliases,
        compiler_params=_cparams(("parallel", "parallel")),
        name="diff_attention",
    )(*args)


def _attn_kernel_aliased(lam_ref, q_ref, kT_ref, v_ref, g_ref, prev_ref, o_ref, *, lam_init):
    del prev_ref
    _attn_kernel(lam_ref, q_ref, kT_ref, v_ref, g_ref, o_ref, lam_init=lam_init)


def _ssd_kernel(sched_ref, xbc_ref, dt_ref, dtT_ref, dtb_ref, dtbT_ref, ac_ref, acT_ref, dsk_ref, h0_ref,
                y_ref, hfin_ref, h_scr):
    s = pl.program_id(0)
    ns = pl.num_programs(0)
    fwd = sched_ref[ns + s] == 0
    first = sched_ref[2 * ns + s] == 1
    last = sched_ref[3 * ns + s] == 1
    Q = SSM_CHUNK

    @pl.when(first)
    def _():
        h_scr[...] = h0_ref[0, 0]

    xbc = xbc_ref[...]
    xs = xbc[:, :SSM_W]
    dt = _softplus(dt_ref[...] + dtb_ref[...])
    dtT = _softplus(dtT_ref[...] + dtbT_ref[...])
    a = dt * ac_ref[...]
    aT = dtT * acT_ref[...]
    li = lax.broadcasted_iota(jnp.int32, (Q, Q), 0)
    si = lax.broadcasted_iota(jnp.int32, (Q, Q), 1)
    valid = jnp.where(fwd, li - si, si - li) >= 0
    tri = valid.astype(F32)
    cs = jnp.dot(tri, a, preferred_element_type=F32, precision=HIGHEST)
    csT = lax.dot_general(aT, tri, (((1,), (1,)), ((), ())), preferred_element_type=F32, precision=HIGHEST)
    tot = jnp.sum(a, axis=0, keepdims=True)

    def pick(arr, h, axis):
        lo = lax.slice_in_dim(arr, h, h + 1, axis=axis)
        hi = lax.slice_in_dim(arr, N_SSM_HEADS + h, N_SSM_HEADS + h + 1, axis=axis)
        return jnp.where(fwd, lo, hi)

    ys = []
    per_group = N_SSM_HEADS // SSM_GROUPS
    for g in range(SSM_GROUPS):
        bg = xbc[:, SSM_W + g * SSM_STATE:SSM_W + (g + 1) * SSM_STATE]
        cg = xbc[:, SSM_W + SSM_GROUPS * SSM_STATE + g * SSM_STATE:SSM_W + SSM_GROUPS * SSM_STATE + (g + 1) * SSM_STATE]
        cgb = cg.astype(BF16)
        gram = lax.dot_general(cgb, bg.astype(BF16), (((1,), (1,)), ((), ())), preferred_element_type=F32)
        for hh in range(per_group):
            h = g * per_group + hh
            col = pick(cs, h, 1)
            rowv = pick(csT, h, 0)
            dth = pick(dt, h, 1)
            toth = pick(tot, h, 1)
            decay = jnp.exp(jnp.where(valid, col - rowv, -jnp.inf))
            xh = xs[:, h * SSM_HEAD_DIM:(h + 1) * SSM_HEAD_DIM]
            xd = (xh * dth).astype(BF16)
            y = jnp.dot((gram * decay).astype(BF16), xd, preferred_element_type=F32)
            hT = h_scr[h]
            y = y + jnp.dot(cgb, hT.astype(BF16), preferred_element_type=F32) * jnp.exp(col)
            bd = (bg * jnp.exp(toth - col)).astype(BF16)
            upd = lax.dot_general(bd, xd, (((0,), (0,)), ((), ())), preferred_element_type=F32)
            h_scr[h] = jnp.exp(toth) * hT + upd
            ys.append(y)
    y = jnp.concatenate(ys, axis=-1)
    y_ref[0] = y + jnp.where(fwd, 1.0, 0.0) * xs * dsk_ref[...]

    @pl.when(last)
    def _():
        hfin_ref[0, 0] = h_scr[...]


def _ssd(sched, xbc_act, dt, dtT, dt_bias, dt_biasT, a_coef, a_coefT, d_skip_l, h0T, n_seq):
    n = xbc_act.shape[0]
    ns = sched.shape[0] // 5
    Q = SSM_CHUNK
    const = lambda a: pl.BlockSpec(a.shape, lambda s, sc: (0,) * a.ndim)
    state_spec = pl.BlockSpec((1, 1, N_SSM_HEADS, SSM_STATE, SSM_HEAD_DIM),
                              lambda s, sc: (sc[4 * ns + s], sc[ns + s], 0, 0, 0))
    grid_spec = pltpu.PrefetchScalarGridSpec(
        num_scalar_prefetch=1,
        grid=(ns,),
        in_specs=[
            pl.BlockSpec((Q, CONV_CH), lambda s, sc: (sc[s], 0)),
            pl.BlockSpec((Q, 16), lambda s, sc: (sc[s], 0)),
            pl.BlockSpec((16, Q), lambda s, sc: (0, sc[s])),
            const(dt_bias), const(dt_biasT), const(a_coef), const(a_coefT), const(d_skip_l),
            state_spec,
        ],
        out_specs=[
            pl.BlockSpec((1, Q, SSM_W), lambda s, sc: (sc[ns + s], sc[s], 0)),
            state_spec,
        ],
        scratch_shapes=[pltpu.VMEM((N_SSM_HEADS, SSM_STATE, SSM_HEAD_DIM), F32)],
    )
    return pl.pallas_call(
        _ssd_kernel,
        grid_spec=grid_spec,
        out_shape=[
            jax.ShapeDtypeStruct((N_DIRS, n, SSM_W), F32),
            jax.ShapeDtypeStruct((n_seq, N_DIRS, N_SSM_HEADS, SSM_STATE, SSM_HEAD_DIM), F32),
        ],
        compiler_params=_cparams(("arbitrary",)),
        name="ssd_scan",
    )(sched, xbc_act, dt, dtT, dt_bias, dt_biasT, a_coef, a_coefT, d_skip_l, h0T)


def _route_slots(scores, bias):
    tm = scores.shape[1]
    per_group = N_EXPERTS // N_EXPERT_GROUPS
    sel = scores + bias
    io_g = lax.broadcasted_iota(jnp.int32, (per_group, tm), 0)
    gscore = []
    for g in range(N_EXPERT_GROUPS):
        xg = sel[g * per_group:(g + 1) * per_group]
        m1 = jnp.max(xg, axis=0, keepdims=True)
        first = jnp.min(jnp.where(xg == m1, io_g, per_group), axis=0, keepdims=True)
        m2 = jnp.max(jnp.where(io_g == first, -jnp.inf, xg), axis=0, keepdims=True)
        gscore.append(m1 + m2)
    masked = []
    for g in range(N_EXPERT_GROUPS):
        beat = jnp.zeros((1, tm), jnp.int32)
        for o in range(N_EXPERT_GROUPS):
            if o == g:
                continue
            wins = (gscore[o] >= gscore[g]) if o < g else (gscore[o] > gscore[g])
            beat = beat + wins.astype(jnp.int32)
        masked.append(jnp.where(beat < TOPK_GROUPS, sel[g * per_group:(g + 1) * per_group], -jnp.inf))
    selm = jnp.concatenate(masked, axis=0)
    io_e = lax.broadcasted_iota(jnp.int32, (N_EXPERTS, tm), 0)
    rank = jnp.zeros((N_EXPERTS, tm), jnp.int32)
    for o in range(N_EXPERTS):
        row = selm[o:o + 1]
        rank = rank + jnp.where(row > selm, 1, jnp.where(row == selm, (io_e > o).astype(jnp.int32), 0))
    chosen = rank < TOP_K
    denom = jnp.sum(jnp.where(chosen, scores, 0.0), axis=0, keepdims=True)
    wdense = scores / denom * ROUTED_SCALE
    io_k = lax.broadcasted_iota(jnp.int32, (TOP_K, tm), 0)
    ids = jnp.zeros((TOP_K, tm), F32)
    wts = jnp.zeros((TOP_K, tm), F32)
    io_ef = io_e.astype(F32)
    for k in range(TOP_K):
        hit = rank == k
        ids = jnp.where(io_k == k, jnp.sum(jnp.where(hit, io_ef, 0.0), axis=0, keepdims=True), ids)
        wts = jnp.where(io_k == k, jnp.sum(jnp.where(hit, wdense, 0.0), axis=0, keepdims=True), wts)
    return ids.astype(jnp.int32), wts


def _outproj_kernel(x_ref, pool_ref, att_ref, y_ref, z_ref, sg_ref, w_ref, mod_ref, g2_ref, wr_ref, rb_ref,
                    x1_ref, h2_ref, eid_ref, ew_ref):
    y = (y_ref[0] + y_ref[1]) * _silu(z_ref[...])
    half = SSM_W // SSM_GROUPS
    parts = []
    for g in range(SSM_GROUPS):
        yg = y[:, g * half:(g + 1) * half]
        parts.append(yg * lax.rsqrt(jnp.mean(yg * yg, axis=-1, keepdims=True) + EPS))
    ssm = jnp.concatenate(parts, axis=-1) * sg_ref[...]
    cat = jnp.concatenate([pool_ref[...], att_ref[...], ssm], axis=-1).astype(BF16)
    mixed = jnp.dot(cat, w_ref[...], preferred_element_type=F32)
    x1 = x_ref[...] + mod_ref[0, 2:3, :] * mixed
    x1_ref[...] = x1
    h2 = x1 * lax.rsqrt(jnp.mean(x1 * x1, axis=-1, keepdims=True) + EPS) * g2_ref[...]
    h2 = h2 * (1.0 + mod_ref[0, 4:5, :]) + mod_ref[0, 3:4, :]
    h2_ref[...] = h2
    logits = lax.dot_general(wr_ref[...], h2, (((1,), (1,)), ((), ())), preferred_element_type=F32, precision=HIGHEST)
    ids, wts = _route_slots(jax.nn.sigmoid(logits), rb_ref[...])
    eid_ref[...] = ids
    ew_ref[...] = wts


def _out_projection(x, pool_out, att, y2, z, ssm_g, w_out, mod_l, norm2_g, w_rT, r_bias, n_ctx_tiles, lat_tiles):
    n = x.shape[0]
    nt = n // TM

    def mod_row(i):
        return jnp.where(i < n_ctx_tiles, 0, 1 + (i - n_ctx_tiles) // lat_tiles)

    row = lambda w: pl.BlockSpec((TM, w), lambda i: (i, 0))
    const = lambda a: pl.BlockSpec(a.shape, lambda i: (0,) * a.ndim)
    return pl.pallas_call(
        _outproj_kernel,
        grid=(nt,),
        in_specs=[
            row(D_MODEL), row(POOL_W), row(ATT_W),
            pl.BlockSpec((N_DIRS, TM, SSM_W), lambda i: (0, i, 0)),
            row(SSM_W), const(ssm_g), const(w_out),
            pl.BlockSpec((1, 6, D_MODEL), lambda i: (mod_row(i), 0, 0)),
            const(norm2_g), const(w_rT), const(r_bias),
        ],
        out_specs=[row(D_MODEL), row(D_MODEL), pl.BlockSpec((TOP_K, TM), lambda i: (0, i)),
                   pl.BlockSpec((TOP_K, TM), lambda i: (0, i))],
        out_shape=[
            jax.ShapeDtypeStruct((n, D_MODEL), F32),
            jax.ShapeDtypeStruct((n, D_MODEL), F32),
            jax.ShapeDtypeStruct((TOP_K, n), jnp.int32),
            jax.ShapeDtypeStruct((TOP_K, n), F32),
        ],
        compiler_params=_cparams(("parallel",)),
        name="out_projection",
    )(x, pool_out, att, y2, z, ssm_g, w_out, mod_l, norm2_g, w_rT, r_bias)


def _moe_kernel(meta_ref, tok_ref, wrow_ref, x_ref, wg_ref, wu_ref, wd_ref, acc_ref, xg_scr, y_scr):
    b = pl.program_id(0)
    nb = pl.num_programs(0)
    row_lo = meta_ref[3 * nb + b]
    row_hi = meta_ref[4 * nb + b]
    valid = meta_ref[5 * nb + b] == 1
    first = meta_ref[6 * nb + b] == 1
    R = MOE_BLK

    @pl.when(first)
    def _():
        acc_ref[...] = jnp.zeros_like(acc_ref)

    @pl.when(valid)
    def _():
        for r in range(R):
            src = pl.multiple_of(tok_ref[0, 0, r] * SUBLANES, SUBLANES)
            xg_scr[r * SUBLANES:(r + 1) * SUBLANES, :] = x_ref[0, pl.ds(src, SUBLANES), :]
        chunks = [xg_scr[pl.ds(j, R, stride=SUBLANES), :] for j in range(D_MODEL // LANES)]
        xb = jnp.concatenate(chunks, axis=-1).astype(BF16)
        hg = jnp.dot(xb, wg_ref[0], preferred_element_type=F32)
        hu = jnp.dot(xb, wu_ref[0], preferred_element_type=F32)
        hm = (_silu(hg) * hu).astype(BF16)
        y = jnp.dot(hm, wd_ref[0], preferred_element_type=F32)
        for j in range(D_MODEL // LANES):
            y_scr[pl.ds(j, R, stride=SUBLANES), :] = y[:, j * LANES:(j + 1) * LANES]
        group = SUBLANES
        for g0 in range(0, R, group):
            pend = []
            for r in range(g0, g0 + group):
                mine = jnp.logical_and(r >= row_lo, r < row_hi)
                dst_tok = jnp.where(mine, tok_ref[0, 0, r], MOE_TS + r % SUBLANES)
                wgt = jnp.where(mine, wrow_ref[0, 0, r], 0.0)
                dst = pl.multiple_of(dst_tok * SUBLANES, SUBLANES)
                cur = acc_ref[0, pl.ds(dst, SUBLANES), :]
                pend.append((dst, cur + y_scr[r * SUBLANES:(r + 1) * SUBLANES, :] * wgt))
            for dst, val in pend:
                acc_ref[0, pl.ds(dst, SUBLANES), :] = val


def _routed_experts(meta, tok, wrow, x_rows, w_gate, w_up, w_down, n_tiles):
    nb = meta.shape[0] // 7
    R = MOE_BLK
    rows_in = MOE_TS * SUBLANES
    rows_out = (MOE_TS + SUBLANES) * SUBLANES
    smem_blk = lambda: pl.BlockSpec((1, 1, R), lambda b, m: (m[2 * nb + b], 0, 0), memory_space=pltpu.SMEM)
    grid_spec = pltpu.PrefetchScalarGridSpec(
        num_scalar_prefetch=1,
        grid=(nb,),
        in_specs=[
            smem_blk(), smem_blk(),
            pl.BlockSpec((1, rows_in, LANES), lambda b, m: (m[b], 0, 0)),
            pl.BlockSpec((1, D_MODEL, EXPERT_FF), lambda b, m: (m[nb + b], 0, 0)),
            pl.BlockSpec((1, D_MODEL, EXPERT_FF), lambda b, m: (m[nb + b], 0, 0)),
            pl.BlockSpec((1, EXPERT_FF, D_MODEL), lambda b, m: (m[nb + b], 0, 0)),
        ],
        out_specs=pl.BlockSpec((1, rows_out, LANES), lambda b, m: (m[b], 0, 0)),
        scratch_shapes=[pltpu.VMEM((R * SUBLANES, LANES), F32), pltpu.VMEM((R * SUBLANES, LANES), F32)],
    )
    return pl.pallas_call(
        _moe_kernel,
        grid_spec=grid_spec,
        out_shape=jax.ShapeDtypeStruct((n_tiles, rows_out, LANES), F32),
        compiler_params=_cparams(("arbitrary",)),
        name="routed_experts",
    )(meta, tok, wrow, x_rows, w_gate, w_up, w_down)


def _dispatch_plan(eids, ewts):
    n = eids.shape[1]
    s = n * TOP_K
    nkeys = (n // MOE_TS) * N_EXPERTS
    tok = jnp.arange(n, dtype=jnp.int32)
    key = (tok // MOE_TS)[None, :] * N_EXPERTS + eids
    ltok = jnp.broadcast_to((tok % MOE_TS)[None, :], key.shape)
    skey, stok, sw = lax.sort((key.reshape(-1), ltok.reshape(-1), ewts.reshape(-1)), num_keys=1)
    bounds = jnp.searchsorted(skey, jnp.arange(nkeys + 1, dtype=jnp.int32), side="left").astype(jnp.int32)
    start, end = bounds[:-1], bounds[1:]
    first_blk = start // MOE_BLK
    nsteps = jnp.where(end > start, (end - 1) // MOE_BLK - first_blk + 1, 0)
    cum = jnp.cumsum(nsteps)
    nb = s // MOE_BLK + nkeys
    step = jnp.arange(nb, dtype=jnp.int32)
    valid = step < cum[-1]
    sc = jnp.minimum(step, cum[-1] - 1)
    k = jnp.searchsorted(cum, sc, side="right").astype(jnp.int32)
    blk = first_blk[k] + sc - (cum[k] - nsteps[k])
    row_lo = jnp.maximum(start[k], blk * MOE_BLK) - blk * MOE_BLK
    row_hi = jnp.minimum(end[k], (blk + 1) * MOE_BLK) - blk * MOE_BLK
    btile = k // N_EXPERTS
    bexp = k % N_EXPERTS
    first = valid & jnp.concatenate([jnp.ones((1,), bool), btile[1:] != btile[:-1]])
    meta = jnp.concatenate([btile, bexp, blk, row_lo, row_hi, valid.astype(jnp.int32), first.astype(jnp.int32)])
    shp = (s // MOE_BLK, 1, MOE_BLK)
    return meta.astype(jnp.int32), stok.reshape(shp), sw.reshape(shp)


def _ffn_out_kernel(x1_ref, h2_ref, r_ref, wg_ref, wu_ref, wd_ref, mod_ref, o_ref):
    hb = h2_ref[...].astype(BF16)
    hg = jnp.dot(hb, wg_ref[...], preferred_element_type=F32)
    hu = jnp.dot(hb, wu_ref[...], preferred_element_type=F32)
    shared = jnp.dot((_silu(hg) * hu).astype(BF16), wd_ref[...], preferred_element_type=F32)
    o_ref[...] = x1_ref[...] + mod_ref[0, 5:6, :] * (r_ref[...] + shared)


def _ffn_out(x1, h2, routed, ws_gate, ws_up, ws_down, mod_l, n_ctx_tiles, lat_tiles):
    n = x1.shape[0]
    nt = n // TM

    def mod_row(i):
        return jnp.where(i < n_ctx_tiles, 0, 1 + (i - n_ctx_tiles) // lat_tiles)

    row = lambda w: pl.BlockSpec((TM, w), lambda i: (i, 0))
    const = lambda a: pl.BlockSpec(a.shape, lambda i: (0,) * a.ndim)
    return pl.pallas_call(
        _ffn_out_kernel,
        grid=(nt,),
        in_specs=[row(D_MODEL), row(D_MODEL), row(D_MODEL), const(ws_gate), const(ws_up), const(ws_down),
                  pl.BlockSpec((1, 6, D_MODEL), lambda i: (mod_row(i), 0, 0))],
        out_specs=row(D_MODEL),
        out_shape=jax.ShapeDtypeStruct((n, D_MODEL), F32),
        compiler_params=_cparams(("parallel",)),
        name="ffn_out",
    )(x1, h2, routed, ws_gate, ws_up, ws_down, mod_l)


def _rope_tables(t_seq):
    rows = t_seq // GRID_W
    r, col = jnp.meshgrid(jnp.arange(rows), jnp.arange(GRID_W), indexing="ij")
    pos = jnp.stack([r.reshape(-1), col.reshape(-1)], axis=-1).astype(F32)
    inv = ROPE_THETA ** (-jnp.arange(ROPE_FREQS, dtype=F32) / ROPE_FREQS)
    ang = pos[:, :, None] * inv
    cos = jnp.cos(ang).reshape(t_seq, 2 * ROPE_FREQS)
    sin = jnp.sin(ang).reshape(t_seq, 2 * ROPE_FREQS)
    cos32 = jnp.concatenate([cos, cos], axis=-1)
    sin32 = jnp.concatenate([-sin, sin], axis=-1)
    reps = QK_W // ATT_QK_DIM
    cos_t = jnp.tile(cos32, (1, reps))
    sin_t = jnp.tile(sin32, (1, reps))
    cos_t = jnp.concatenate([cos_t, jnp.ones((TM, QK_W), F32)], axis=0)
    sin_t = jnp.concatenate([sin_t, jnp.zeros((TM, QK_W), F32)], axis=0)
    return cos_t, sin_t


def _block_diag(blocks):
    g, a, b = blocks.shape
    out = jnp.zeros((g * a, g * b), blocks.dtype)
    for i in range(g):
        out = out.at[i * a:(i + 1) * a, i * b:(i + 1) * b].set(blocks[i])
    return out


def _ssd_schedule(n_ctx_seq, ctx_chunks, n_lat_seq, lat_chunks):
    blk, dirs, first, last, seq = [], [], [], [], []
    base = 0
    sid = 0
    for n_seq, nc in ((n_ctx_seq, ctx_chunks), (n_lat_seq, lat_chunks)):
        for _ in range(n_seq):
            for d in range(N_DIRS):
                order = range(nc) if d == 0 else range(nc - 1, -1, -1)
                for k, c in enumerate(order):
                    blk.append(base + c)
                    dirs.append(d)
                    first.append(int(k == 0))
                    last.append(int(k == nc - 1))
                    seq.append(sid)
            base += nc
            sid += 1
    return jnp.asarray(blk + dirs + first + last + seq, jnp.int32)


def kernel(x_prompt, x_sample, cache_k, cache_v, state_ssm, c, c_ctx, w_mod, b_mod, norm1_g, w_in, pool_w, pool_scale, q_norm_g, k_norm_g, lambda_q, lambda_k, subln_g, conv_w, conv_b, dt_bias, a_log, d_skip, ssm_norm_g, w_out, norm2_g, w_router, router_bias, w_gate, w_up, w_down, ws_gate, ws_up, ws_down):
    nb_ctx, t_ctx, d = x_prompt.shape
    nb_lat, t_lat, _ = x_sample.shape
    depth = w_mod.shape[0]
    past = cache_k.shape[2]
    n_ctx = nb_ctx * t_ctx
    n_lat = nb_lat * t_lat
    n = n_ctx + n_lat
    assert d == D_MODEL and t_ctx % TM == 0 and t_lat % TM == 0 and n % MOE_TS == 0
    assert 1 + nb_lat <= SUBLANES
    n_ctx_tiles = n_ctx // TM
    lat_tiles = t_lat // TM
    n_seq = nb_ctx + nb_lat

    x = jnp.concatenate([x_prompt.reshape(n_ctx, d), x_sample.reshape(n_lat, d)], axis=0)
    cond = jnp.zeros((SUBLANES, d), F32).at[0].set(c_ctx).at[1:1 + nb_lat].set(c)
    mod = _modulation(cond, w_mod, b_mod).reshape(depth, SUBLANES, 6, d)

    cos_t, sin_t = _rope_tables(t_lat)
    ones_bd = _block_diag(jnp.ones((QK_W // ATT_QK_DIM, ATT_QK_DIM, ATT_QK_DIM), F32))
    seq_tiles = jnp.asarray([t_ctx // TM] * n_ctx_tiles + [lat_tiles] * (nb_lat * lat_tiles), jnp.int32)
    tile_pos = jnp.asarray([i % (t_ctx // TM) for i in range(n_ctx_tiles)] + [i % lat_tiles for i in range(nb_lat * lat_tiles)], jnp.int32)
    seq_info = jnp.concatenate([seq_tiles, tile_pos])
    sched = _ssd_schedule(nb_ctx, t_ctx // SSM_CHUNK, nb_lat, t_lat // SSM_CHUNK)

    new_k, new_v, new_ssm = [], [], []
    for l in range(depth):
        mod_l = mod[l]
        w_in_p = jnp.pad(w_in[l], ((0, 0), (0, IN_W_PAD - IN_W))).astype(BF16)
        w_dtT = w_in[l][:, IN_W - 16:].T.astype(BF16)
        qg = jnp.tile(q_norm_g[l], QK_W // ATT_QK_DIM)[None]
        kg = jnp.tile(k_norm_g[l], QK_W // ATT_QK_DIM)[None]
        pool_in, qn, kT, kn, v, z, xbc, dt, dtT = _in_projection(
            x, mod_l, norm1_g[l][None], w_in_p, w_dtT, qg, kg, cos_t, sin_t, ones_bd, n_ctx_tiles, lat_tiles)

        wbd = _block_diag(pool_w[l]).astype(BF16)
        pool_out, xbc_act = _token_shift(seq_info, pool_in, xbc, wbd, pool_scale[l][None], conv_w[l], conv_b[l][None])

        lam_init = 0.8 - 0.6 * math.exp(-0.3 * l)
        lq = lambda_q[l].astype(F32)
        lk = lambda_k[l].astype(F32)
        lam = (jnp.exp(jnp.sum(lq[0] * lk[0])) - jnp.exp(jnp.sum(lq[1] * lk[1])) + lam_init).reshape(1)
        vb = v.astype(BF16)
        kT_ctx = kT[:, :n_ctx].reshape(QK_W, nb_ctx, t_ctx).transpose(1, 0, 2)
        v_ctx = vb[:n_ctx].reshape(nb_ctx, t_ctx, N_ATT_HEADS, ATT_V_DIM).transpose(0, 2, 1, 3)
        kT_lat = kT[:, n_ctx:].reshape(QK_W, nb_lat, t_lat).transpose(1, 0, 2)
        ck = cache_k[:, l].reshape(nb_lat, past, QK_W).transpose(0, 2, 1).astype(BF16)
        kT_lat = jnp.concatenate([kT_lat, ck], axis=2)
        v_lat = vb[n_ctx:].reshape(nb_lat, t_lat, N_ATT_HEADS, ATT_V_DIM).transpose(0, 2, 1, 3)
        cv = cache_v[:, l].astype(BF16).transpose(0, 2, 1, 3)
        v_lat = jnp.concatenate([v_lat, cv], axis=2)
        sg = subln_g[l][None]
        att = _attention(lam, qn, kT_ctx, v_ctx, sg, 0, nb_ctx, t_ctx, None, lam_init, n)
        att = _attention(lam, qn, kT_lat, v_lat, sg, n_ctx, nb_lat, t_lat, att, lam_init, n)

        h0T = jnp.concatenate([
            jnp.zeros((nb_ctx, N_DIRS, N_SSM_HEADS, SSM_STATE, SSM_HEAD_DIM), F32),
            state_ssm[:, l].astype(F32).transpose(0, 1, 2, 4, 3)], axis=0)
        dtb = dt_bias[l].reshape(1, N_DIRS * N_SSM_HEADS)
        ac = (-jnp.exp(a_log[l].astype(F32))).reshape(1, N_DIRS * N_SSM_HEADS)
        dsk = jnp.repeat(d_skip[l].astype(F32), SSM_HEAD_DIM)[None]
        y2, hfinT = _ssd(sched, xbc_act, dt, dtT, dtb, dtb.T, ac, ac.T, dsk, h0T, n_seq)

        x1, h2, eids, ewts = _out_projection(
            x, pool_out, att, y2, z, ssm_norm_g[l][None], w_out[l].astype(BF16), mod_l, norm2_g[l][None],
            w_router[l].T, router_bias[l].astype(F32)[:, None], n_ctx_tiles, lat_tiles)

        meta, tok_list, w_list = _dispatch_plan(eids, ewts)
        n_tiles = n // MOE_TS
        x_rows = h2.reshape(n_tiles, MOE_TS * SUBLANES, LANES)
        acc = _routed_experts(meta, tok_list, w_list, x_rows, w_gate[l].astype(BF16), w_up[l].astype(BF16),
                              w_down[l].astype(BF16), n_tiles)
        routed = acc[:, :MOE_TS * SUBLANES].reshape(n, d)
        x = _ffn_out(x1, h2, routed, ws_gate[l].astype(BF16), ws_up[l].astype(BF16), ws_down[l].astype(BF16),
                     mod_l, n_ctx_tiles, lat_tiles)

        new_k.append(kn[:n_ctx].reshape(nb_ctx, t_ctx, N_ATT_HEADS, 2, ATT_QK_DIM))
        new_v.append(v[:n_ctx].reshape(nb_ctx, t_ctx, N_ATT_HEADS, ATT_V_DIM))
        new_ssm.append(hfinT[:nb_ctx].transpose(0, 1, 2, 4, 3))

    y_prompt = x[:n_ctx].reshape(nb_ctx, t_ctx, d)
    y_sample = x[n_ctx:].reshape(nb_lat, t_lat, d)
    return (y_prompt, y_sample, jnp.stack(new_k, axis=1), jnp.stack(new_v, axis=1), jnp.stack(new_ssm, axis=1))
```

```python
import functools
import math

import jax
import jax.numpy as jnp
from jax import lax
from jax.experimental import pallas as pl
from jax.experimental.pallas import tpu as pltpu

F32 = jnp.float32
BF16 = jnp.bfloat16

D_MODEL = 1024
GRID_W = 64
POOL_W = 256
POOL_WINDOWS = (2, 4, 8, 16)
POOL_GROUP = 64
N_ATT_HEADS = 4
ATT_V_DIM = 64
ATT_QK_DIM = 32
ATT_W = 256
QK_W = 256
ROPE_FREQS = 8
ROPE_THETA = 10000.0
SSM_W = 512
SSM_HEAD_DIM = 64
N_SSM_HEADS = 8
SSM_GROUPS = 2
SSM_STATE = 64
SSM_CONV = 5
SSM_CHUNK = 128
N_DIRS = 2
CONV_CH = 768
N_EXPERTS = 64
N_EXPERT_GROUPS = 8
TOPK_GROUPS = 4
TOP_K = 8
EXPERT_FF = 256
ROUTED_SCALE = 2.5
EPS = 1e-6
IN_W = 2320
IN_W_PAD = 2432

LANES = 128
SUBLANES = 8
TM = 256
HALO = 8
MOE_TS = 4096
MOE_CHAIN = 128
MOE_NCHAIN = 2
MOE_BLK = MOE_CHAIN * MOE_NCHAIN
MOE_RMW_GROUP = 8
VMEM_LIMIT = 48 * 1024 * 1024

HIGHEST = lax.Precision.HIGHEST


def _cparams(sem):
    return pltpu.CompilerParams(dimension_semantics=sem, vmem_limit_bytes=VMEM_LIMIT)


def _silu(x):
    return x * jax.nn.sigmoid(x)


def _softplus(x):
    return jnp.maximum(x, 0.0) + jnp.log1p(jnp.exp(-jnp.abs(x)))


def _mod_kernel(c_ref, w_ref, b_ref, o_ref):
    c = c_ref[...]
    o_ref[0] = jnp.dot(_silu(c), w_ref[0], preferred_element_type=F32, precision=HIGHEST) + b_ref[0]


def _modulation(cond, w_mod, b_mod):
    L = w_mod.shape[0]
    bn = 1536
    return pl.pallas_call(
        _mod_kernel,
        grid=(L, 6 * D_MODEL // bn),
        in_specs=[
            pl.BlockSpec((SUBLANES, D_MODEL), lambda l, j: (0, 0)),
            pl.BlockSpec((1, D_MODEL, bn), lambda l, j: (l, 0, j)),
            pl.BlockSpec((1, 1, bn), lambda l, j: (l, 0, j)),
        ],
        out_specs=pl.BlockSpec((1, SUBLANES, bn), lambda l, j: (l, 0, j)),
        out_shape=jax.ShapeDtypeStruct((L, SUBLANES, 6 * D_MODEL), F32),
        compiler_params=_cparams(("parallel", "parallel")),
        name="adaln_mod",
    )(cond, w_mod, b_mod.reshape(L, 1, 6 * D_MODEL))


def _inproj_kernel(x_ref, mod_ref, g_ref, w_ref, wdt_ref, qg_ref, kg_ref, cos_ref, sin_ref, ones_ref,
                   pool_ref, qn_ref, kT_ref, kn_ref, v_ref, z_ref, xbc_ref, dt_ref, dtT_ref):
    x = x_ref[...]
    xn = x * lax.rsqrt(jnp.mean(x * x, axis=-1, keepdims=True) + EPS) * g_ref[...]
    h = xn * (1.0 + mod_ref[0, 1:2, :]) + mod_ref[0, 0:1, :]
    hb = h.astype(BF16)
    u = jnp.dot(hb, w_ref[...], preferred_element_type=F32)
    pool_ref[...] = u[:, 0:256]
    v_ref[...] = u[:, 768:1024]
    z_ref[...] = u[:, 1024:1536]
    xbc_ref[...] = u[:, 1536:2304]
    dt_ref[...] = u[:, 2304:2304 + LANES][:, 0:16]
    dtT_ref[...] = lax.dot_general(wdt_ref[...], hb, (((1,), (1,)), ((), ())), preferred_element_type=F32)

    lane = lax.broadcasted_iota(jnp.int32, (TM, QK_W), 1)
    first_half = (lane % ATT_QK_DIM) < (ATT_QK_DIM // 2)
    cos = cos_ref[...]
    sin = sin_ref[...]

    def qk_norm(t, gain):
        ss = jnp.dot(t * t, ones_ref[...], preferred_element_type=F32, precision=HIGHEST)
        return t * lax.rsqrt(ss * (1.0 / ATT_QK_DIM) + EPS) * gain

    def rope(t):
        partner = jnp.where(first_half, pltpu.roll(t, QK_W - ATT_QK_DIM // 2, 1), pltpu.roll(t, ATT_QK_DIM // 2, 1))
        return t * cos + partner * sin

    qn_ref[...] = rope(qk_norm(u[:, 256:512], qg_ref[...])).astype(BF16)
    kn = qk_norm(u[:, 512:768], kg_ref[...])
    kn_ref[...] = kn
    kT_ref[...] = rope(kn).T.astype(BF16)


def _in_projection(x, mod_l, norm_g, w_in_p, w_dtT, qg, kg, cos_t, sin_t, ones_bd, n_ctx_tiles, lat_tiles):
    n = x.shape[0]
    nt = n // TM

    def mod_row(i):
        return jnp.where(i < n_ctx_tiles, 0, 1 + (i - n_ctx_tiles) // lat_tiles)

    def rope_row(i):
        return jnp.where(i < n_ctx_tiles, lat_tiles, (i - n_ctx_tiles) % lat_tiles)

    row = lambda w: pl.BlockSpec((TM, w), lambda i: (i, 0))
    const = lambda a: pl.BlockSpec(a.shape, lambda i: (0,) * a.ndim)
    outs = [
        jax.ShapeDtypeStruct((n, POOL_W), F32),
        jax.ShapeDtypeStruct((n, QK_W), BF16),
        jax.ShapeDtypeStruct((QK_W, n), BF16),
        jax.ShapeDtypeStruct((n, QK_W), F32),
        jax.ShapeDtypeStruct((n, ATT_W), F32),
        jax.ShapeDtypeStruct((n, SSM_W), F32),
        jax.ShapeDtypeStruct((n, CONV_CH), F32),
        jax.ShapeDtypeStruct((n, 16), F32),
        jax.ShapeDtypeStruct((16, n), F32),
    ]
    return pl.pallas_call(
        _inproj_kernel,
        grid=(nt,),
        in_specs=[
            row(D_MODEL),
            pl.BlockSpec((1, 6, D_MODEL), lambda i: (mod_row(i), 0, 0)),
            const(norm_g), const(w_in_p), const(w_dtT), const(qg), const(kg),
            pl.BlockSpec((TM, QK_W), lambda i: (rope_row(i), 0)),
            pl.BlockSpec((TM, QK_W), lambda i: (rope_row(i), 0)),
            const(ones_bd),
        ],
        out_specs=[
            row(POOL_W), row(QK_W), pl.BlockSpec((QK_W, TM), lambda i: (0, i)), row(QK_W), row(ATT_W),
            row(SSM_W), row(CONV_CH), row(16), pl.BlockSpec((16, TM), lambda i: (0, i)),
        ],
        out_shape=outs,
        compiler_params=_cparams(("parallel",)),
        name="in_projection",
    )(x, mod_l, norm_g, w_in_p, w_dtT, qg, kg, cos_t, sin_t, ones_bd)


def _shift_kernel(seq_tiles_ref, pc_ref, pp_ref, pn_ref, xc_ref, xp_ref, xn_ref, wbd_ref, psc_ref, cw_ref, cb_ref,
                  pool_ref, xbc_ref):
    i = pl.program_id(0)
    st = seq_tiles_ref[i]
    pos = seq_tiles_ref[i + pl.num_programs(0)]
    has_prev = pos > 0
    has_next = pos < st - 1
    rows = TM + 2 * HALO
    r = lax.broadcasted_iota(jnp.int32, (rows, 1), 0)
    t = r - HALO + pos * TM
    t_len = st * TM

    def extended(c_ref, p_ref, n_ref):
        return jnp.concatenate([jnp.where(has_prev, p_ref[...], 0.0), c_ref[...],
                                jnp.where(has_next, n_ref[...], 0.0)], axis=0)

    def down(a, s):
        return pltpu.roll(a, s, 0)

    def up(a, s):
        return pltpu.roll(a, rows - s, 0)

    u = extended(pc_ref, pp_ref, pn_ref)
    w2 = u + down(u, 1)
    w4 = down(w2, 1) + up(w2, 1)
    w8 = down(w4, 2) + up(w4, 2)
    w16 = down(w8, 4) + up(w8, 4)
    lane = lax.broadcasted_iota(jnp.int32, (rows, POOL_W), 1)
    grp = lane // POOL_GROUP
    wsum = jnp.where(grp == 0, w2, jnp.where(grp == 1, w4, jnp.where(grp == 2, w8, w16)))
    half = jnp.where(grp == 0, 1, jnp.where(grp == 1, 2, jnp.where(grp == 2, 4, 8)))
    lo = jnp.maximum(t - half, 0)
    hi = jnp.minimum(t + half - 1, t_len - 1)
    cnt = (hi - lo + 1).astype(F32)
    pooled = (wsum / cnt - u)[HALO:HALO + TM]
    mixed = jnp.dot(pooled.astype(BF16), wbd_ref[...], preferred_element_type=F32)
    pool_ref[...] = mixed * psc_ref[...]

    x = extended(xc_ref, xp_ref, xn_ref)
    acc = x * cw_ref[2:3, :] + cb_ref[...]
    acc = acc + down(x, 2) * cw_ref[0:1, :] + down(x, 1) * cw_ref[1:2, :]
    acc = acc + up(x, 1) * cw_ref[3:4, :] + up(x, 2) * cw_ref[4:5, :]
    xbc_ref[...] = _silu(acc[HALO:HALO + TM])


def _token_shift(seq_info, pool_in, xbc, wbd, pool_scale, conv_w, conv_b):
    n = pool_in.shape[0]
    nt = n // TM
    hb = TM // HALO
    nhb = n // HALO

    cur = lambda w: pl.BlockSpec((TM, w), lambda i, s: (i, 0))
    prev = lambda w: pl.BlockSpec((HALO, w), lambda i, s: (jnp.maximum(i * hb - 1, 0), 0))
    nxt = lambda w: pl.BlockSpec((HALO, w), lambda i, s: (jnp.minimum((i + 1) * hb, nhb - 1), 0))
    const = lambda a: pl.BlockSpec(a.shape, lambda i, s: (0,) * a.ndim)
    grid_spec = pltpu.PrefetchScalarGridSpec(
        num_scalar_prefetch=1,
        grid=(nt,),
        in_specs=[cur(POOL_W), prev(POOL_W), nxt(POOL_W), cur(CONV_CH), prev(CONV_CH), nxt(CONV_CH),
                  const(wbd), const(pool_scale), const(conv_w), const(conv_b)],
        out_specs=[cur(POOL_W), cur(CONV_CH)],
    )
    return pl.pallas_call(
        _shift_kernel,
        grid_spec=grid_spec,
        out_shape=[jax.ShapeDtypeStruct((n, POOL_W), F32), jax.ShapeDtypeStruct((n, CONV_CH), F32)],
        compiler_params=_cparams(("parallel",)),
        name="token_shift",
    )(seq_info, pool_in, pool_in, pool_in, xbc, xbc, xbc, wbd, pool_scale, conv_w, conv_b)


def _attn_kernel(lam_ref, q_ref, kT_ref, v_ref, g_ref, o_ref, *, lam_init):
    lam = lam_ref[0]
    c0 = (ATT_QK_DIM ** -0.5) * math.log2(math.e)
    q = q_ref[...]
    outs = []
    for h in range(N_ATT_HEADS):
        parts = []
        for c in range(2):
            off = h * 2 * ATT_QK_DIM + c * ATT_QK_DIM
            s = jnp.dot(q[:, off:off + ATT_QK_DIM], kT_ref[0, off:off + ATT_QK_DIM, :], preferred_element_type=F32)
            m = jnp.max(s, axis=-1, keepdims=True)
            p = jnp.exp2((s - m) * c0)
            pv = jnp.dot(p.astype(BF16), v_ref[0, h], preferred_element_type=F32)
            parts.append(pv * (1.0 / jnp.sum(p, axis=-1, keepdims=True)))
        o = parts[0] - lam * parts[1]
        o = o * lax.rsqrt(jnp.mean(o * o, axis=-1, keepdims=True) + EPS) * g_ref[...] * (1.0 - lam_init)
        outs.append(o)
    o_ref[...] = jnp.concatenate(outs, axis=-1)


def _attention(lam, qn, kT, vh, subln_g, row0, n_seq, t_seq, lam_init):
    s_all = kT.shape[-1]
    tq = TM
    qb = t_seq // tq
    b0 = row0 // tq
    return pl.pallas_call(
        functools.partial(_attn_kernel, lam_init=lam_init),
        grid=(n_seq, qb),
        in_specs=[
            pl.BlockSpec(memory_space=pltpu.SMEM),
            pl.BlockSpec((tq, QK_W), lambda b, i: (b0 + b * qb + i, 0)),
            pl.BlockSpec((1, QK_W, s_all), lambda b, i: (b, 0, 0)),
            pl.BlockSpec((1, N_ATT_HEADS, s_all, ATT_V_DIM), lambda b, i: (b, 0, 0, 0)),
            pl.BlockSpec((1, ATT_V_DIM), lambda b, i: (0, 0)),
        ],
        out_specs=pl.BlockSpec((tq, ATT_W), lambda b, i: (b * qb + i, 0)),
        out_shape=jax.ShapeDtypeStruct((n_seq * t_seq, ATT_W), F32),
        compiler_params=_cparams(("parallel", "parallel")),
        name="diff_attention",
    )(lam, qn, kT, vh, subln_g)


def _ssd_kernel(sched_ref, xbc_ref, dt_ref, dtT_ref, dtb_ref, dtbT_ref, ac_ref, acT_ref, dsk_ref, h0_ref,
                y_ref, hfin_ref, h_scr):
    s = pl.program_id(0)
    ns = pl.num_programs(0)
    fwd = sched_ref[ns + s] == 0
    first = sched_ref[2 * ns + s] == 1
    last = sched_ref[3 * ns + s] == 1
    Q = SSM_CHUNK

    @pl.when(first)
    def _():
        h_scr[...] = h0_ref[0, 0]

    xbc = xbc_ref[...]
    xs = xbc[:, :SSM_W]
    dt = _softplus(dt_ref[...] + dtb_ref[...])
    dtT = _softplus(dtT_ref[...] + dtbT_ref[...])
    a = dt * ac_ref[...]
    aT = dtT * acT_ref[...]
    li = lax.broadcasted_iota(jnp.int32, (Q, Q), 0)
    si = lax.broadcasted_iota(jnp.int32, (Q, Q), 1)
    valid = jnp.where(fwd, li - si, si - li) >= 0
    tri = valid.astype(F32)
    cs = jnp.dot(tri, a, preferred_element_type=F32, precision=HIGHEST)
    csT = lax.dot_general(aT, tri, (((1,), (1,)), ((), ())), preferred_element_type=F32, precision=HIGHEST)
    H = N_SSM_HEADS
    csT = jnp.where(fwd, csT[:H], csT[H:])
    dtTd = jnp.where(fwd, dtT[:H], dtT[H:])
    totT = jnp.dot(jnp.where(fwd, aT[:H], aT[H:]), jnp.ones((Q, Q), F32), preferred_element_type=F32,
                   precision=HIGHEST)
    cs_d = jnp.where(fwd, cs[:, :H], cs[:, H:])
    eye = (lax.broadcasted_iota(jnp.int32, (SSM_STATE, SSM_STATE), 0)
           == lax.broadcasted_iota(jnp.int32, (SSM_STATE, SSM_STATE), 1)).astype(F32)

    ys = []
    per_group = N_SSM_HEADS // SSM_GROUPS
    for g in range(SSM_GROUPS):
        bg = xbc[:, SSM_W + g * SSM_STATE:SSM_W + (g + 1) * SSM_STATE]
        cg = xbc[:, SSM_W + SSM_GROUPS * SSM_STATE + g * SSM_STATE:SSM_W + SSM_GROUPS * SSM_STATE + (g + 1) * SSM_STATE]
        cgb = cg.astype(BF16)
        gram = lax.dot_general(cgb, bg.astype(BF16), (((1,), (1,)), ((), ())), preferred_element_type=F32)
        bgT = lax.dot_general(eye, bg, (((1,), (1,)), ((), ())), preferred_element_type=F32, precision=HIGHEST)
        for hh in range(per_group):
            h = g * per_group + hh
            col = jnp.broadcast_to(cs_d[:, h:h + 1], (Q, Q))
            rowv = csT[h:h + 1]
            dtr = dtTd[h:h + 1]
            totr = totT[h:h + 1]
            decay = jnp.exp(jnp.where(valid, col - rowv, -jnp.inf))
            xh = xs[:, h * SSM_HEAD_DIM:(h + 1) * SSM_HEAD_DIM].astype(BF16)
            y = jnp.dot((gram * decay * dtr).astype(BF16), xh, preferred_element_type=F32)
            hT = h_scr[h]
            y = y + jnp.dot(cgb, hT.astype(BF16), preferred_element_type=F32) * jnp.exp(col[:, :SSM_HEAD_DIM])
            bdT = (bgT * (jnp.exp(totr - rowv) * dtr)).astype(BF16)
            upd = jnp.dot(bdT, xh, preferred_element_type=F32)
            h_scr[h] = jnp.exp(totr[:, :SSM_HEAD_DIM]) * hT + upd
            ys.append(y)
    y = jnp.concatenate(ys, axis=-1)
    y_ref[0] = y + jnp.where(fwd, 1.0, 0.0) * xs * dsk_ref[...]

    @pl.when(last)
    def _():
        hfin_ref[0, 0] = h_scr[...]


def _ssd(sched, xbc_act, dt, dtT, dt_bias, dt_biasT, a_coef, a_coefT, d_skip_l, h0T, n_seq):
    n = xbc_act.shape[0]
    ns = sched.shape[0] // 5
    Q = SSM_CHUNK
    const = lambda a: pl.BlockSpec(a.shape, lambda s, sc: (0,) * a.ndim)
    state_spec = pl.BlockSpec((1, 1, N_SSM_HEADS, SSM_STATE, SSM_HEAD_DIM),
                              lambda s, sc: (sc[4 * ns + s], sc[ns + s], 0, 0, 0))
    grid_spec = pltpu.PrefetchScalarGridSpec(
        num_scalar_prefetch=1,
        grid=(ns,),
        in_specs=[
            pl.BlockSpec((Q, CONV_CH), lambda s, sc: (sc[s], 0)),
            pl.BlockSpec((Q, 16), lambda s, sc: (sc[s], 0)),
            pl.BlockSpec((16, Q), lambda s, sc: (0, sc[s])),
            const(dt_bias), const(dt_biasT), const(a_coef), const(a_coefT), const(d_skip_l),
            state_spec,
        ],
        out_specs=[
            pl.BlockSpec((1, Q, SSM_W), lambda s, sc: (sc[ns + s], sc[s], 0)),
            state_spec,
        ],
        scratch_shapes=[pltpu.VMEM((N_SSM_HEADS, SSM_STATE, SSM_HEAD_DIM), F32)],
    )
    return pl.pallas_call(
        _ssd_kernel,
        grid_spec=grid_spec,
        out_shape=[
            jax.ShapeDtypeStruct((N_DIRS, n, SSM_W), F32),
            jax.ShapeDtypeStruct((n_seq, N_DIRS, N_SSM_HEADS, SSM_STATE, SSM_HEAD_DIM), F32),
        ],
        compiler_params=_cparams(("arbitrary",)),
        name="ssd_scan",
    )(sched, xbc_act, dt, dtT, dt_bias, dt_biasT, a_coef, a_coefT, d_skip_l, h0T)


def _route_slots(scores, bias):
    tm = scores.shape[1]
    per_group = N_EXPERTS // N_EXPERT_GROUPS
    sel = scores + bias
    io_g = lax.broadcasted_iota(jnp.int32, (per_group, tm), 0)
    gscore = []
    for g in range(N_EXPERT_GROUPS):
        xg = sel[g * per_group:(g + 1) * per_group]
        m1 = jnp.max(xg, axis=0, keepdims=True)
        first = jnp.min(jnp.where(xg == m1, io_g, per_group), axis=0, keepdims=True)
        m2 = jnp.max(jnp.where(io_g == first, -jnp.inf, xg), axis=0, keepdims=True)
        gscore.append(m1 + m2)
    masked = []
    for g in range(N_EXPERT_GROUPS):
        beat = jnp.zeros((1, tm), jnp.int32)
        for o in range(N_EXPERT_GROUPS):
            if o == g:
                continue
            wins = (gscore[o] >= gscore[g]) if o < g else (gscore[o] > gscore[g])
            beat = beat + wins.astype(jnp.int32)
        masked.append(jnp.where(beat < TOPK_GROUPS, sel[g * per_group:(g + 1) * per_group], -jnp.inf))
    selm = jnp.concatenate(masked, axis=0)
    io_e = lax.broadcasted_iota(jnp.int32, (N_EXPERTS, tm), 0)
    rank = jnp.zeros((N_EXPERTS, tm), jnp.int32)
    for o in range(N_EXPERTS):
        row = selm[o:o + 1]
        rank = rank + jnp.where(row > selm, 1, jnp.where(row == selm, (io_e > o).astype(jnp.int32), 0))
    chosen = rank < TOP_K
    denom = jnp.sum(jnp.where(chosen, scores, 0.0), axis=0, keepdims=True)
    wdense = scores / denom * ROUTED_SCALE
    io_k = lax.broadcasted_iota(jnp.int32, (TOP_K, tm), 0)
    ids = jnp.zeros((TOP_K, tm), F32)
    wts = jnp.zeros((TOP_K, tm), F32)
    io_ef = io_e.astype(F32)
    for k in range(TOP_K):
        hit = rank == k
        ids = jnp.where(io_k == k, jnp.sum(jnp.where(hit, io_ef, 0.0), axis=0, keepdims=True), ids)
        wts = jnp.where(io_k == k, jnp.sum(jnp.where(hit, wdense, 0.0), axis=0, keepdims=True), wts)
    return ids.astype(jnp.int32), wts


def _outproj_kernel(x_ref, pool_ref, attc_ref, attl_ref, y_ref, z_ref, sg_ref, w_ref, mod_ref, g2_ref, wr_ref, rb_ref,
                    x1_ref, h2_ref, eid_ref, ew_ref, *, n_ctx_tiles):
    att = jnp.where(pl.program_id(0) < n_ctx_tiles, attc_ref[...], attl_ref[...])
    y = (y_ref[0] + y_ref[1]) * _silu(z_ref[...])
    half = SSM_W // SSM_GROUPS
    parts = []
    for g in range(SSM_GROUPS):
        yg = y[:, g * half:(g + 1) * half]
        parts.append(yg * lax.rsqrt(jnp.mean(yg * yg, axis=-1, keepdims=True) + EPS))
    ssm = jnp.concatenate(parts, axis=-1) * sg_ref[...]
    cat = jnp.concatenate([pool_ref[...], att, ssm], axis=-1).astype(BF16)
    mixed = jnp.dot(cat, w_ref[...], preferred_element_type=F32)
    x1 = x_ref[...] + mod_ref[0, 2:3, :] * mixed
    x1_ref[...] = x1
    h2 = x1 * lax.rsqrt(jnp.mean(x1 * x1, axis=-1, keepdims=True) + EPS) * g2_ref[...]
    h2 = h2 * (1.0 + mod_ref[0, 4:5, :]) + mod_ref[0, 3:4, :]
    h2_ref[...] = h2
    logits = lax.dot_general(wr_ref[...], h2, (((1,), (1,)), ((), ())), preferred_element_type=F32, precision=HIGHEST)
    ids, wts = _route_slots(jax.nn.sigmoid(logits), rb_ref[...])
    eid_ref[...] = ids
    ew_ref[...] = wts


def _out_projection(x, pool_out, att_ctx, att_lat, y2, z, ssm_g, w_out, mod_l, norm2_g, w_rT, r_bias, n_ctx_tiles,
                    lat_tiles):
    n = x.shape[0]
    nt = n // TM

    def mod_row(i):
        return jnp.where(i < n_ctx_tiles, 0, 1 + (i - n_ctx_tiles) // lat_tiles)

    row = lambda w: pl.BlockSpec((TM, w), lambda i: (i, 0))
    const = lambda a: pl.BlockSpec(a.shape, lambda i: (0,) * a.ndim)
    return pl.pallas_call(
        functools.partial(_outproj_kernel, n_ctx_tiles=n_ctx_tiles),
        grid=(nt,),
        in_specs=[
            row(D_MODEL), row(POOL_W),
            pl.BlockSpec((TM, ATT_W), lambda i: (jnp.minimum(i, n_ctx_tiles - 1), 0)),
            pl.BlockSpec((TM, ATT_W), lambda i: (jnp.maximum(i - n_ctx_tiles, 0), 0)),
            pl.BlockSpec((N_DIRS, TM, SSM_W), lambda i: (0, i, 0)),
            row(SSM_W), const(ssm_g), const(w_out),
            pl.BlockSpec((1, 6, D_MODEL), lambda i: (mod_row(i), 0, 0)),
            const(norm2_g), const(w_rT), const(r_bias),
        ],
        out_specs=[row(D_MODEL), row(D_MODEL), pl.BlockSpec((TOP_K, TM), lambda i: (0, i)),
                   pl.BlockSpec((TOP_K, TM), lambda i: (0, i))],
        out_shape=[
            jax.ShapeDtypeStruct((n, D_MODEL), F32),
            jax.ShapeDtypeStruct((n, D_MODEL), F32),
            jax.ShapeDtypeStruct((TOP_K, n), jnp.int32),
            jax.ShapeDtypeStruct((TOP_K, n), F32),
        ],
        compiler_params=_cparams(("parallel",)),
        name="out_projection",
    )(x, pool_out, att_ctx, att_lat, y2, z, ssm_g, w_out, mod_l, norm2_g, w_rT, r_bias)


def _moe_kernel(meta_ref, tok_ref, wrow_ref, x_ref, wgu_ref, wd_ref, acc_ref, *scr):
    b = pl.program_id(0)
    nb = pl.num_programs(0)
    valid = meta_ref[3 * nb + b] == 1
    first = meta_ref[4 * nb + b] == 1
    C = MOE_CHAIN
    xg_scrs, y_scrs = scr[:MOE_NCHAIN], scr[MOE_NCHAIN:]

    @pl.when(first)
    def _():
        acc_ref[...] = jnp.zeros_like(acc_ref)

    @pl.when(valid)
    def _():
        ii = lax.broadcasted_iota(jnp.int32, (C, C), 0)
        jj = lax.broadcasted_iota(jnp.int32, (C, C), 1)
        for c in range(MOE_NCHAIN):
            xg, ys = xg_scrs[c], y_scrs[c]
            for r in range(C):
                tok = jnp.minimum(tok_ref[0, 0, c * C + r], MOE_TS - 1)
                src = pl.multiple_of(tok * SUBLANES, SUBLANES)
                xg[r * SUBLANES:(r + 1) * SUBLANES, :] = x_ref[0, pl.ds(src, SUBLANES), :]
            chunks = [xg[pl.ds(j, C, stride=SUBLANES), :] for j in range(D_MODEL // LANES)]
            xb = jnp.concatenate(chunks, axis=-1).astype(BF16)
            hgu = jnp.dot(xb, wgu_ref[0], preferred_element_type=F32)
            hm = (_silu(hgu[:, :EXPERT_FF]) * hgu[:, EXPERT_FF:]).astype(BF16)
            y = jnp.dot(hm, wd_ref[0], preferred_element_type=F32)
            w_row = jnp.broadcast_to(wrow_ref[0, :, c * C:(c + 1) * C], (C, C))
            w_col = jnp.sum(jnp.where(ii == jj, w_row, 0.0), axis=1, keepdims=True)
            y = y * w_col
            for j in range(D_MODEL // LANES):
                ys[pl.ds(j, C, stride=SUBLANES), :] = y[:, j * LANES:(j + 1) * LANES]
        for c in range(MOE_NCHAIN):
            ys = y_scrs[c]
            for g0 in range(0, C, MOE_RMW_GROUP):
                pend = []
                for r in range(g0, g0 + MOE_RMW_GROUP):
                    dst = pl.multiple_of(tok_ref[0, 0, c * C + r] * SUBLANES, SUBLANES)
                    pend.append((dst, acc_ref[0, pl.ds(dst, SUBLANES), :] + ys[r * SUBLANES:(r + 1) * SUBLANES, :]))
                for dst, val in pend:
                    acc_ref[0, pl.ds(dst, SUBLANES), :] = val


def _routed_experts(meta, tok, wrow, x_rows, w_gateup, w_down, n_tiles):
    nb = meta.shape[0] // 5
    R = MOE_BLK
    C = MOE_CHAIN
    rows_in = MOE_TS * SUBLANES
    rows_out = (MOE_TS + SUBLANES) * SUBLANES
    grid_spec = pltpu.PrefetchScalarGridSpec(
        num_scalar_prefetch=1,
        grid=(nb,),
        in_specs=[
            pl.BlockSpec((1, 1, R), lambda b, m: (m[2 * nb + b], 0, 0), memory_space=pltpu.SMEM),
            pl.BlockSpec((1, 1, R), lambda b, m: (m[2 * nb + b], 0, 0)),
            pl.BlockSpec((1, rows_in, LANES), lambda b, m: (m[b], 0, 0), pipeline_mode=pl.Buffered(1)),
            pl.BlockSpec((1, D_MODEL, 2 * EXPERT_FF), lambda b, m: (m[nb + b], 0, 0)),
            pl.BlockSpec((1, EXPERT_FF, D_MODEL), lambda b, m: (m[nb + b], 0, 0)),
        ],
        out_specs=pl.BlockSpec((1, rows_out, LANES), lambda b, m: (m[b], 0, 0), pipeline_mode=pl.Buffered(1)),
        scratch_shapes=[pltpu.VMEM((C * SUBLANES, LANES), F32) for _ in range(2 * MOE_NCHAIN)],
    )
    return pl.pallas_call(
        _moe_kernel,
        grid_spec=grid_spec,
        out_shape=jax.ShapeDtypeStruct((n_tiles, rows_out, LANES), F32),
        compiler_params=_cparams(("arbitrary",)),
        name="routed_experts",
    )(meta, tok, wrow, x_rows, w_gateup, w_down)


def _dispatch_plan(eids, ewts):
    n = eids.shape[1]
    s = n * TOP_K
    n_tiles = n // MOE_TS
    nkeys = n_tiles * N_EXPERTS
    tok_bits = (MOE_TS + SUBLANES - 1).bit_length()
    tok = jnp.arange(n, dtype=jnp.int32)
    key = (tok // MOE_TS)[None, :] * N_EXPERTS + eids
    experts = jnp.arange(N_EXPERTS, dtype=jnp.int32)[:, None, None]
    counts = jnp.sum((eids[None] == experts).reshape(N_EXPERTS, TOP_K, n_tiles, MOE_TS), axis=(1, 3), dtype=jnp.int32)
    counts = counts.T.reshape(nkeys)
    padded = (counts + MOE_BLK - 1) // MOE_BLK * MOE_BLK
    j = jnp.arange(MOE_BLK - 1, dtype=jnp.int32)[None, :]
    keys = jnp.arange(nkeys, dtype=jnp.int32)[:, None]
    pad_code = (keys << (tok_bits + 1)) | (1 << tok_bits) | (MOE_TS + j % SUBLANES)
    pad_code = jnp.where(j < (padded - counts)[:, None], pad_code, jnp.iinfo(jnp.int32).max)
    code = (key << (tok_bits + 1)) | (tok % MOE_TS)[None, :]
    n_slots = s + nkeys * (MOE_BLK - 1)
    nb = -(-n_slots // MOE_BLK)
    tail = nb * MOE_BLK - n_slots
    all_code = jnp.concatenate([code.reshape(-1), pad_code.reshape(-1),
                                jnp.full((tail,), jnp.iinfo(jnp.int32).max, jnp.int32)])
    all_w = jnp.concatenate([ewts.reshape(-1), jnp.zeros((nkeys * (MOE_BLK - 1) + tail,), F32)])
    scode, sw = lax.sort((all_code, all_w), num_keys=1)
    stok = scode & ((1 << tok_bits) - 1)
    pend = jnp.cumsum(padded)
    used = pend[-1] // MOE_BLK
    step = jnp.arange(nb, dtype=jnp.int32)
    valid = step < used
    blk = jnp.minimum(step, used - 1)
    k = jnp.searchsorted(pend, blk * MOE_BLK, side="right").astype(jnp.int32)
    btile = k // N_EXPERTS
    bexp = k % N_EXPERTS
    first = valid & jnp.concatenate([jnp.ones((1,), bool), btile[1:] != btile[:-1]])
    meta = jnp.concatenate([btile, bexp, blk, valid.astype(jnp.int32), first.astype(jnp.int32)])
    shp = (nb, 1, MOE_BLK)
    return meta.astype(jnp.int32), stok.reshape(shp), sw.reshape(shp)


def _ffn_out_kernel(x1_ref, h2_ref, r_ref, wg_ref, wu_ref, wd_ref, mod_ref, o_ref):
    hb = h2_ref[...].astype(BF16)
    hg = jnp.dot(hb, wg_ref[...], preferred_element_type=F32)
    hu = jnp.dot(hb, wu_ref[...], preferred_element_type=F32)
    shared = jnp.dot((_silu(hg) * hu).astype(BF16), wd_ref[...], preferred_element_type=F32)
    o_ref[...] = x1_ref[...] + mod_ref[0, 5:6, :] * (r_ref[...] + shared)


def _ffn_out(x1, h2, routed, ws_gate, ws_up, ws_down, mod_l, n_ctx_tiles, lat_tiles):
    n = x1.shape[0]
    nt = n // TM

    def mod_row(i):
        return jnp.where(i < n_ctx_tiles, 0, 1 + (i - n_ctx_tiles) // lat_tiles)

    row = lambda w: pl.BlockSpec((TM, w), lambda i: (i, 0))
    const = lambda a: pl.BlockSpec(a.shape, lambda i: (0,) * a.ndim)
    return pl.pallas_call(
        _ffn_out_kernel,
        grid=(nt,),
        in_specs=[row(D_MODEL), row(D_MODEL), row(D_MODEL), const(ws_gate), const(ws_up), const(ws_down),
                  pl.BlockSpec((1, 6, D_MODEL), lambda i: (mod_row(i), 0, 0))],
        out_specs=row(D_MODEL),
        out_shape=jax.ShapeDtypeStruct((n, D_MODEL), F32),
        compiler_params=_cparams(("parallel",)),
        name="ffn_out",
    )(x1, h2, routed, ws_gate, ws_up, ws_down, mod_l)


def _rope_tables(t_seq):
    rows = t_seq // GRID_W
    r, col = jnp.meshgrid(jnp.arange(rows), jnp.arange(GRID_W), indexing="ij")
    pos = jnp.stack([r.reshape(-1), col.reshape(-1)], axis=-1).astype(F32)
    inv = ROPE_THETA ** (-jnp.arange(ROPE_FREQS, dtype=F32) / ROPE_FREQS)
    ang = pos[:, :, None] * inv
    cos = jnp.cos(ang).reshape(t_seq, 2 * ROPE_FREQS)
    sin = jnp.sin(ang).reshape(t_seq, 2 * ROPE_FREQS)
    cos32 = jnp.concatenate([cos, cos], axis=-1)
    sin32 = jnp.concatenate([-sin, sin], axis=-1)
    reps = QK_W // ATT_QK_DIM
    cos_t = jnp.tile(cos32, (1, reps))
    sin_t = jnp.tile(sin32, (1, reps))
    cos_t = jnp.concatenate([cos_t, jnp.ones((TM, QK_W), F32)], axis=0)
    sin_t = jnp.concatenate([sin_t, jnp.zeros((TM, QK_W), F32)], axis=0)
    return cos_t, sin_t


def _block_diag(blocks):
    g, a, b = blocks.shape
    out = jnp.zeros((g * a, g * b), blocks.dtype)
    for i in range(g):
        out = out.at[i * a:(i + 1) * a, i * b:(i + 1) * b].set(blocks[i])
    return out


def _ssd_schedule(n_ctx_seq, ctx_chunks, n_lat_seq, lat_chunks):
    blk, dirs, first, last, seq = [], [], [], [], []
    base = 0
    sid = 0
    for n_seq, nc in ((n_ctx_seq, ctx_chunks), (n_lat_seq, lat_chunks)):
        for _ in range(n_seq):
            for d in range(N_DIRS):
                order = range(nc) if d == 0 else range(nc - 1, -1, -1)
                for k, c in enumerate(order):
                    blk.append(base + c)
                    dirs.append(d)
                    first.append(int(k == 0))
                    last.append(int(k == nc - 1))
                    seq.append(sid)
            base += nc
            sid += 1
    return jnp.asarray(blk + dirs + first + last + seq, jnp.int32)


def kernel(x_prompt, x_sample, cache_k, cache_v, state_ssm, c, c_ctx, w_mod, b_mod, norm1_g, w_in, pool_w, pool_scale, q_norm_g, k_norm_g, lambda_q, lambda_k, subln_g, conv_w, conv_b, dt_bias, a_log, d_skip, ssm_norm_g, w_out, norm2_g, w_router, router_bias, w_gate, w_up, w_down, ws_gate, ws_up, ws_down):
    nb_ctx, t_ctx, d = x_prompt.shape
    nb_lat, t_lat, _ = x_sample.shape
    depth = w_mod.shape[0]
    past = cache_k.shape[2]
    n_ctx = nb_ctx * t_ctx
    n_lat = nb_lat * t_lat
    n = n_ctx + n_lat
    assert d == D_MODEL and t_ctx % TM == 0 and t_lat % TM == 0 and n % MOE_TS == 0
    assert 1 + nb_lat <= SUBLANES
    n_ctx_tiles = n_ctx // TM
    lat_tiles = t_lat // TM
    n_seq = nb_ctx + nb_lat

    x = jnp.concatenate([x_prompt.reshape(n_ctx, d), x_sample.reshape(n_lat, d)], axis=0)
    cond = jnp.zeros((SUBLANES, d), F32).at[0].set(c_ctx).at[1:1 + nb_lat].set(c)
    mod = _modulation(cond, w_mod, b_mod).reshape(depth, SUBLANES, 6, d)

    cos_t, sin_t = _rope_tables(t_lat)
    ones_bd = _block_diag(jnp.ones((QK_W // ATT_QK_DIM, ATT_QK_DIM, ATT_QK_DIM), F32))
    seq_tiles = jnp.asarray([t_ctx // TM] * n_ctx_tiles + [lat_tiles] * (nb_lat * lat_tiles), jnp.int32)
    tile_pos = jnp.asarray([i % (t_ctx // TM) for i in range(n_ctx_tiles)] + [i % lat_tiles for i in range(nb_lat * lat_tiles)], jnp.int32)
    seq_info = jnp.concatenate([seq_tiles, tile_pos])
    sched = _ssd_schedule(nb_ctx, t_ctx // SSM_CHUNK, nb_lat, t_lat // SSM_CHUNK)

    new_k, new_v, new_ssm = [], [], []
    for l in range(depth):
        mod_l = mod[l]
        w_in_p = jnp.pad(w_in[l], ((0, 0), (0, IN_W_PAD - IN_W))).astype(BF16)
        w_dtT = w_in[l][:, IN_W - 16:].T.astype(BF16)
        qg = jnp.tile(q_norm_g[l], QK_W // ATT_QK_DIM)[None]
        kg = jnp.tile(k_norm_g[l], QK_W // ATT_QK_DIM)[None]
        pool_in, qn, kT, kn, v, z, xbc, dt, dtT = _in_projection(
            x, mod_l, norm1_g[l][None], w_in_p, w_dtT, qg, kg, cos_t, sin_t, ones_bd, n_ctx_tiles, lat_tiles)

        wbd = _block_diag(pool_w[l]).astype(BF16)
        pool_out, xbc_act = _token_shift(seq_info, pool_in, xbc, wbd, pool_scale[l][None], conv_w[l], conv_b[l][None])

        lam_init = 0.8 - 0.6 * math.exp(-0.3 * l)
        lq = lambda_q[l].astype(F32)
        lk = lambda_k[l].astype(F32)
        lam = (jnp.exp(jnp.sum(lq[0] * lk[0])) - jnp.exp(jnp.sum(lq[1] * lk[1])) + lam_init).reshape(1)
        vb = v.astype(BF16)
        kT_ctx = kT[:, :n_ctx].reshape(QK_W, nb_ctx, t_ctx).transpose(1, 0, 2)
        v_ctx = vb[:n_ctx].reshape(nb_ctx, t_ctx, N_ATT_HEADS, ATT_V_DIM).transpose(0, 2, 1, 3)
        kT_lat = kT[:, n_ctx:].reshape(QK_W, nb_lat, t_lat).transpose(1, 0, 2)
        ck = cache_k[:, l].reshape(nb_lat, past, QK_W).transpose(0, 2, 1).astype(BF16)
        kT_lat = jnp.concatenate([kT_lat, ck], axis=2)
        v_lat = vb[n_ctx:].reshape(nb_lat, t_lat, N_ATT_HEADS, ATT_V_DIM).transpose(0, 2, 1, 3)
        cv = cache_v[:, l].astype(BF16).transpose(0, 2, 1, 3)
        v_lat = jnp.concatenate([v_lat, cv], axis=2)
        sg = subln_g[l][None]
        att_ctx = _attention(lam, qn, kT_ctx, v_ctx, sg, 0, nb_ctx, t_ctx, lam_init)
        att_lat = _attention(lam, qn, kT_lat, v_lat, sg, n_ctx, nb_lat, t_lat, lam_init)

        h0T = jnp.concatenate([
            jnp.zeros((nb_ctx, N_DIRS, N_SSM_HEADS, SSM_STATE, SSM_HEAD_DIM), F32),
            state_ssm[:, l].astype(F32).transpose(0, 1, 2, 4, 3)], axis=0)
        dtb = dt_bias[l].reshape(1, N_DIRS * N_SSM_HEADS)
        ac = (-jnp.exp(a_log[l].astype(F32))).reshape(1, N_DIRS * N_SSM_HEADS)
        dsk = jnp.repeat(d_skip[l].astype(F32), SSM_HEAD_DIM)[None]
        y2, hfinT = _ssd(sched, xbc_act, dt, dtT, dtb, dtb.T, ac, ac.T, dsk, h0T, n_seq)

        x1, h2, eids, ewts = _out_projection(
            x, pool_out, att_ctx, att_lat, y2, z, ssm_norm_g[l][None], w_out[l].astype(BF16), mod_l, norm2_g[l][None],
            w_router[l].T, router_bias[l].astype(F32)[:, None], n_ctx_tiles, lat_tiles)

        meta, tok_list, w_list = _dispatch_plan(eids, ewts)
        n_tiles = n // MOE_TS
        x_rows = h2.reshape(n_tiles, MOE_TS * SUBLANES, LANES)
        w_gateup = jnp.concatenate([w_gate[l].astype(BF16), w_up[l].astype(BF16)], axis=-1)
        acc = _routed_experts(meta, tok_list, w_list, x_rows, w_gateup, w_down[l].astype(BF16), n_tiles)
        routed = acc[:, :MOE_TS * SUBLANES].reshape(n, d)
        x = _ffn_out(x1, h2, routed, ws_gate[l].astype(BF16), ws_up[l].astype(BF16), ws_down[l].astype(BF16),
                     mod_l, n_ctx_tiles, lat_tiles)

        new_k.append(kn[:n_ctx].reshape(nb_ctx, t_ctx, N_ATT_HEADS, 2, ATT_QK_DIM))
        new_v.append(v[:n_ctx].reshape(nb_ctx, t_ctx, N_ATT_HEADS, ATT_V_DIM))
        new_ssm.append(hfinT[:nb_ctx].transpose(0, 1, 2, 4, 3))

    y_prompt = x[:n_ctx].reshape(nb_ctx, t_ctx, d)
    y_sample = x[n_ctx:].reshape(nb_lat, t_lat, d)
    return (y_prompt, y_sample, jnp.stack(new_k, axis=1), jnp.stack(new_v, axis=1), jnp.stack(new_ssm, axis=1))
```

```python
import functools
import math

import jax
import jax.numpy as jnp
from jax import lax
from jax.experimental import pallas as pl
from jax.experimental.pallas import tpu as pltpu

F32 = jnp.float32
BF16 = jnp.bfloat16

D_MODEL = 1024
GRID_W = 64
POOL_W = 256
POOL_WINDOWS = (2, 4, 8, 16)
POOL_GROUP = 64
N_ATT_HEADS = 4
ATT_V_DIM = 64
ATT_QK_DIM = 32
ATT_W = 256
QK_W = 256
ROPE_FREQS = 8
ROPE_THETA = 10000.0
SSM_W = 512
SSM_HEAD_DIM = 64
N_SSM_HEADS = 8
SSM_GROUPS = 2
SSM_STATE = 64
SSM_CONV = 5
SSM_CHUNK = 128
N_DIRS = 2
CONV_CH = 768
N_EXPERTS = 64
N_EXPERT_GROUPS = 8
TOPK_GROUPS = 4
TOP_K = 8
EXPERT_FF = 256
ROUTED_SCALE = 2.5
EPS = 1e-6
IN_W = 2320
IN_W_PAD = 2432

LANES = 128
SUBLANES = 8
TM = 256
HALO = 8
MOE_TS = 4096
MOE_CHAIN = 128
MOE_NCHAIN = 2
MOE_BLK = MOE_CHAIN * MOE_NCHAIN
MOE_RMW_GROUP = 8
VMEM_LIMIT = 48 * 1024 * 1024

HIGHEST = lax.Precision.HIGHEST


def _cparams(sem):
    return pltpu.CompilerParams(dimension_semantics=sem, vmem_limit_bytes=VMEM_LIMIT)


def _silu(x):
    return x * jax.nn.sigmoid(x)


def _softplus(x):
    return jnp.maximum(x, 0.0) + jnp.log1p(jnp.exp(-jnp.abs(x)))


def _mod_kernel(c_ref, w_ref, b_ref, o_ref):
    c = c_ref[...]
    o_ref[0] = jnp.dot(_silu(c), w_ref[0], preferred_element_type=F32, precision=HIGHEST) + b_ref[0]


def _modulation(cond, w_mod, b_mod):
    L = w_mod.shape[0]
    bn = 1536
    return pl.pallas_call(
        _mod_kernel,
        grid=(L, 6 * D_MODEL // bn),
        in_specs=[
            pl.BlockSpec((SUBLANES, D_MODEL), lambda l, j: (0, 0)),
            pl.BlockSpec((1, D_MODEL, bn), lambda l, j: (l, 0, j)),
            pl.BlockSpec((1, 1, bn), lambda l, j: (l, 0, j)),
        ],
        out_specs=pl.BlockSpec((1, SUBLANES, bn), lambda l, j: (l, 0, j)),
        out_shape=jax.ShapeDtypeStruct((L, SUBLANES, 6 * D_MODEL), F32),
        compiler_params=_cparams(("parallel", "parallel")),
        name="adaln_mod",
    )(cond, w_mod, b_mod.reshape(L, 1, 6 * D_MODEL))


def _inproj_kernel(x_ref, mod_ref, g_ref, w_ref, wdt_ref, qg_ref, kg_ref, cos_ref, sin_ref, ones_ref,
                   pool_ref, qn_ref, kT_ref, kn_ref, v_ref, z_ref, xbc_ref, dt_ref, dtT_ref):
    x = x_ref[...]
    xn = x * lax.rsqrt(jnp.mean(x * x, axis=-1, keepdims=True) + EPS) * g_ref[...]
    h = xn * (1.0 + mod_ref[0, 1:2, :]) + mod_ref[0, 0:1, :]
    hb = h.astype(BF16)
    u = jnp.dot(hb, w_ref[...], preferred_element_type=F32)
    pool_ref[...] = u[:, 0:256]
    v_ref[...] = u[:, 768:1024]
    z_ref[...] = u[:, 1024:1536]
    xbc_ref[...] = u[:, 1536:2304]
    dt_ref[...] = u[:, 2304:2304 + LANES][:, 0:16]
    dtT_ref[...] = lax.dot_general(wdt_ref[...], hb, (((1,), (1,)), ((), ())), preferred_element_type=F32)

    lane = lax.broadcasted_iota(jnp.int32, (TM, QK_W), 1)
    first_half = (lane % ATT_QK_DIM) < (ATT_QK_DIM // 2)
    cos = cos_ref[...]
    sin = sin_ref[...]

    def qk_norm(t, gain):
        ss = jnp.dot(t * t, ones_ref[...], preferred_element_type=F32, precision=HIGHEST)
        return t * lax.rsqrt(ss * (1.0 / ATT_QK_DIM) + EPS) * gain

    def rope(t):
        partner = jnp.where(first_half, pltpu.roll(t, QK_W - ATT_QK_DIM // 2, 1), pltpu.roll(t, ATT_QK_DIM // 2, 1))
        return t * cos + partner * sin

    qn_ref[...] = rope(qk_norm(u[:, 256:512], qg_ref[...])).astype(BF16)
    kn = qk_norm(u[:, 512:768], kg_ref[...])
    kn_ref[...] = kn
    kT_ref[...] = rope(kn).T.astype(BF16)


def _in_projection(x, mod_l, norm_g, w_in_p, w_dtT, qg, kg, cos_t, sin_t, ones_bd, n_ctx_tiles, lat_tiles):
    n = x.shape[0]
    nt = n // TM

    def mod_row(i):
        return jnp.where(i < n_ctx_tiles, 0, 1 + (i - n_ctx_tiles) // lat_tiles)

    def rope_row(i):
        return jnp.where(i < n_ctx_tiles, lat_tiles, (i - n_ctx_tiles) % lat_tiles)

    row = lambda w: pl.BlockSpec((TM, w), lambda i: (i, 0))
    const = lambda a: pl.BlockSpec(a.shape, lambda i: (0,) * a.ndim)
    outs = [
        jax.ShapeDtypeStruct((n, POOL_W), F32),
        jax.ShapeDtypeStruct((n, QK_W), BF16),
        jax.ShapeDtypeStruct((QK_W, n), BF16),
        jax.ShapeDtypeStruct((n, QK_W), F32),
        jax.ShapeDtypeStruct((n, ATT_W), F32),
        jax.ShapeDtypeStruct((n, SSM_W), F32),
        jax.ShapeDtypeStruct((n, CONV_CH), F32),
        jax.ShapeDtypeStruct((n, 16), F32),
        jax.ShapeDtypeStruct((16, n), F32),
    ]
    return pl.pallas_call(
        _inproj_kernel,
        grid=(nt,),
        in_specs=[
            row(D_MODEL),
            pl.BlockSpec((1, 6, D_MODEL), lambda i: (mod_row(i), 0, 0)),
            const(norm_g), const(w_in_p), const(w_dtT), const(qg), const(kg),
            pl.BlockSpec((TM, QK_W), lambda i: (rope_row(i), 0)),
            pl.BlockSpec((TM, QK_W), lambda i: (rope_row(i), 0)),
            const(ones_bd),
        ],
        out_specs=[
            row(POOL_W), row(QK_W), pl.BlockSpec((QK_W, TM), lambda i: (0, i)), row(QK_W), row(ATT_W),
            row(SSM_W), row(CONV_CH), row(16), pl.BlockSpec((16, TM), lambda i: (0, i)),
        ],
        out_shape=outs,
        compiler_params=_cparams(("parallel",)),
        name="in_projection",
    )(x, mod_l, norm_g, w_in_p, w_dtT, qg, kg, cos_t, sin_t, ones_bd)


def _shift_kernel(seq_tiles_ref, pc_ref, pp_ref, pn_ref, xc_ref, xp_ref, xn_ref, wbd_ref, psc_ref, cw_ref, cb_ref,
                  pool_ref, xbc_ref):
    i = pl.program_id(0)
    st = seq_tiles_ref[i]
    pos = seq_tiles_ref[i + pl.num_programs(0)]
    has_prev = pos > 0
    has_next = pos < st - 1
    rows = TM + 2 * HALO
    r = lax.broadcasted_iota(jnp.int32, (rows, 1), 0)
    t = r - HALO + pos * TM
    t_len = st * TM

    def extended(c_ref, p_ref, n_ref):
        return jnp.concatenate([jnp.where(has_prev, p_ref[...], 0.0), c_ref[...],
                                jnp.where(has_next, n_ref[...], 0.0)], axis=0)

    def down(a, s):
        return pltpu.roll(a, s, 0)

    def up(a, s):
        return pltpu.roll(a, rows - s, 0)

    u = extended(pc_ref, pp_ref, pn_ref)
    w2 = u + down(u, 1)
    w4 = down(w2, 1) + up(w2, 1)
    w8 = down(w4, 2) + up(w4, 2)
    w16 = down(w8, 4) + up(w8, 4)
    lane = lax.broadcasted_iota(jnp.int32, (rows, POOL_W), 1)
    grp = lane // POOL_GROUP
    wsum = jnp.where(grp == 0, w2, jnp.where(grp == 1, w4, jnp.where(grp == 2, w8, w16)))
    half = jnp.where(grp == 0, 1, jnp.where(grp == 1, 2, jnp.where(grp == 2, 4, 8)))
    lo = jnp.maximum(t - half, 0)
    hi = jnp.minimum(t + half - 1, t_len - 1)
    cnt = (hi - lo + 1).astype(F32)
    pooled = (wsum / cnt - u)[HALO:HALO + TM]
    mixed = jnp.dot(pooled.astype(BF16), wbd_ref[...], preferred_element_type=F32)
    pool_ref[...] = mixed * psc_ref[...]

    x = extended(xc_ref, xp_ref, xn_ref)
    acc = x * cw_ref[2:3, :] + cb_ref[...]
    acc = acc + down(x, 2) * cw_ref[0:1, :] + down(x, 1) * cw_ref[1:2, :]
    acc = acc + up(x, 1) * cw_ref[3:4, :] + up(x, 2) * cw_ref[4:5, :]
    xbc_ref[...] = _silu(acc[HALO:HALO + TM])


def _token_shift(seq_info, pool_in, xbc, wbd, pool_scale, conv_w, conv_b):
    n = pool_in.shape[0]
    nt = n // TM
    hb = TM // HALO
    nhb = n // HALO

    cur = lambda w: pl.BlockSpec((TM, w), lambda i, s: (i, 0))
    prev = lambda w: pl.BlockSpec((HALO, w), lambda i, s: (jnp.maximum(i * hb - 1, 0), 0))
    nxt = lambda w: pl.BlockSpec((HALO, w), lambda i, s: (jnp.minimum((i + 1) * hb, nhb - 1), 0))
    const = lambda a: pl.BlockSpec(a.shape, lambda i, s: (0,) * a.ndim)
    grid_spec = pltpu.PrefetchScalarGridSpec(
        num_scalar_prefetch=1,
        grid=(nt,),
        in_specs=[cur(POOL_W), prev(POOL_W), nxt(POOL_W), cur(CONV_CH), prev(CONV_CH), nxt(CONV_CH),
                  const(wbd), const(pool_scale), const(conv_w), const(conv_b)],
        out_specs=[cur(POOL_W), cur(CONV_CH)],
    )
    return pl.pallas_call(
        _shift_kernel,
        grid_spec=grid_spec,
        out_shape=[jax.ShapeDtypeStruct((n, POOL_W), F32), jax.ShapeDtypeStruct((n, CONV_CH), F32)],
        compiler_params=_cparams(("parallel",)),
        name="token_shift",
    )(seq_info, pool_in, pool_in, pool_in, xbc, xbc, xbc, wbd, pool_scale, conv_w, conv_b)


def _attn_kernel(lam_ref, q_ref, kT_ref, v_ref, g_ref, o_ref, *, lam_init):
    lam = lam_ref[0]
    c0 = (ATT_QK_DIM ** -0.5) * math.log2(math.e)
    q = q_ref[...]
    outs = []
    for h in range(N_ATT_HEADS):
        parts = []
        for c in range(2):
            off = h * 2 * ATT_QK_DIM + c * ATT_QK_DIM
            s = jnp.dot(q[:, off:off + ATT_QK_DIM], kT_ref[0, off:off + ATT_QK_DIM, :], preferred_element_type=F32)
            m = jnp.max(s, axis=-1, keepdims=True)
            p = jnp.exp2((s - m) * c0)
            pv = jnp.dot(p.astype(BF16), v_ref[0, h], preferred_element_type=F32)
            parts.append(pv * (1.0 / jnp.sum(p, axis=-1, keepdims=True)))
        o = parts[0] - lam * parts[1]
        o = o * lax.rsqrt(jnp.mean(o * o, axis=-1, keepdims=True) + EPS) * g_ref[...] * (1.0 - lam_init)
        outs.append(o)
    o_ref[...] = jnp.concatenate(outs, axis=-1)


def _attention(lam, qn, kT, vh, subln_g, row0, n_seq, t_seq, lam_init):
    s_all = kT.shape[-1]
    tq = TM
    qb = t_seq // tq
    b0 = row0 // tq
    return pl.pallas_call(
        functools.partial(_attn_kernel, lam_init=lam_init),
        grid=(n_seq, qb),
        in_specs=[
            pl.BlockSpec(memory_space=pltpu.SMEM),
            pl.BlockSpec((tq, QK_W), lambda b, i: (b0 + b * qb + i, 0)),
            pl.BlockSpec((1, QK_W, s_all), lambda b, i: (b, 0, 0)),
            pl.BlockSpec((1, N_ATT_HEADS, s_all, ATT_V_DIM), lambda b, i: (b, 0, 0, 0)),
            pl.BlockSpec((1, ATT_V_DIM), lambda b, i: (0, 0)),
        ],
        out_specs=pl.BlockSpec((tq, ATT_W), lambda b, i: (b * qb + i, 0)),
        out_shape=jax.ShapeDtypeStruct((n_seq * t_seq, ATT_W), F32),
        compiler_params=_cparams(("parallel", "parallel")),
        name="diff_attention",
    )(lam, qn, kT, vh, subln_g)


def _ssd_kernel(sched_ref, xbc_ref, dt_ref, dtT_ref, dtb_ref, dtbT_ref, ac_ref, acT_ref, dsk_ref, h0_ref,
                y_ref, hfin_ref, h_scr):
    s = pl.program_id(0)
    ns = pl.num_programs(0)
    fwd = sched_ref[ns + s] == 0
    first = sched_ref[2 * ns + s] == 1
    last = sched_ref[3 * ns + s] == 1
    Q = SSM_CHUNK

    @pl.when(first)
    def _():
        h_scr[...] = h0_ref[0, 0]

    xbc = xbc_ref[...]
    xs = xbc[:, :SSM_W]
    dt = _softplus(dt_ref[...] + dtb_ref[...])
    dtT = _softplus(dtT_ref[...] + dtbT_ref[...])
    a = dt * ac_ref[...]
    aT = dtT * acT_ref[...]
    li = lax.broadcasted_iota(jnp.int32, (Q, Q), 0)
    si = lax.broadcasted_iota(jnp.int32, (Q, Q), 1)
    valid = jnp.where(fwd, li - si, si - li) >= 0
    tri = valid.astype(F32)
    cs = jnp.dot(tri, a, preferred_element_type=F32, precision=HIGHEST)
    csT = lax.dot_general(aT, tri, (((1,), (1,)), ((), ())), preferred_element_type=F32, precision=HIGHEST)
    H = N_SSM_HEADS
    csT = jnp.where(fwd, csT[:H], csT[H:])
    dtTd = jnp.where(fwd, dtT[:H], dtT[H:])
    totT = jnp.dot(jnp.where(fwd, aT[:H], aT[H:]), jnp.ones((Q, Q), F32), preferred_element_type=F32,
                   precision=HIGHEST)
    cs_d = jnp.where(fwd, cs[:, :H], cs[:, H:])
    eye = (lax.broadcasted_iota(jnp.int32, (SSM_STATE, SSM_STATE), 0)
           == lax.broadcasted_iota(jnp.int32, (SSM_STATE, SSM_STATE), 1)).astype(F32)

    ys = []
    per_group = N_SSM_HEADS // SSM_GROUPS
    for g in range(SSM_GROUPS):
        bg = xbc[:, SSM_W + g * SSM_STATE:SSM_W + (g + 1) * SSM_STATE]
        cg = xbc[:, SSM_W + SSM_GROUPS * SSM_STATE + g * SSM_STATE:SSM_W + SSM_GROUPS * SSM_STATE + (g + 1) * SSM_STATE]
        cgb = cg.astype(BF16)
        gram = lax.dot_general(cgb, bg.astype(BF16), (((1,), (1,)), ((), ())), preferred_element_type=F32)
        bgT = lax.dot_general(eye, bg, (((1,), (1,)), ((), ())), preferred_element_type=F32, precision=HIGHEST)
        for hh in range(per_group):
            h = g * per_group + hh
            col = jnp.broadcast_to(cs_d[:, h:h + 1], (Q, Q))
            rowv = csT[h:h + 1]
            dtr = dtTd[h:h + 1]
            totr = totT[h:h + 1]
            decay = jnp.exp(jnp.where(valid, col - rowv, -jnp.inf))
            xh = xs[:, h * SSM_HEAD_DIM:(h + 1) * SSM_HEAD_DIM].astype(BF16)
            y = jnp.dot((gram * decay * dtr).astype(BF16), xh, preferred_element_type=F32)
            hT = h_scr[h]
            y = y + jnp.dot(cgb, hT.astype(BF16), preferred_element_type=F32) * jnp.exp(col[:, :SSM_HEAD_DIM])
            bdT = (bgT * (jnp.exp(totr - rowv) * dtr)).astype(BF16)
            upd = jnp.dot(bdT, xh, preferred_element_type=F32)
            h_scr[h] = jnp.exp(totr[:, :SSM_HEAD_DIM]) * hT + upd
            ys.append(y)
    y = jnp.concatenate(ys, axis=-1)
    y_ref[0] = y + jnp.where(fwd, 1.0, 0.0) * xs * dsk_ref[...]

    @pl.when(last)
    def _():
        hfin_ref[0, 0] = h_scr[...]


def _ssd(sched, xbc_act, dt, dtT, dt_bias, dt_biasT, a_coef, a_coefT, d_skip_l, h0T, n_seq):
    n = xbc_act.shape[0]
    ns = sched.shape[0] // 5
    Q = SSM_CHUNK
    const = lambda a: pl.BlockSpec(a.shape, lambda s, sc: (0,) * a.ndim)
    state_spec = pl.BlockSpec((1, 1, N_SSM_HEADS, SSM_STATE, SSM_HEAD_DIM),
                              lambda s, sc: (sc[4 * ns + s], sc[ns + s], 0, 0, 0))
    grid_spec = pltpu.PrefetchScalarGridSpec(
        num_scalar_prefetch=1,
        grid=(ns,),
        in_specs=[
            pl.BlockSpec((Q, CONV_CH), lambda s, sc: (sc[s], 0)),
            pl.BlockSpec((Q, 16), lambda s, sc: (sc[s], 0)),
            pl.BlockSpec((16, Q), lambda s, sc: (0, sc[s])),
            const(dt_bias), const(dt_biasT), const(a_coef), const(a_coefT), const(d_skip_l),
            state_spec,
        ],
        out_specs=[
            pl.BlockSpec((1, Q, SSM_W), lambda s, sc: (sc[ns + s], sc[s], 0)),
            state_spec,
        ],
        scratch_shapes=[pltpu.VMEM((N_SSM_HEADS, SSM_STATE, SSM_HEAD_DIM), F32)],
    )
    return pl.pallas_call(
        _ssd_kernel,
        grid_spec=grid_spec,
        out_shape=[
            jax.ShapeDtypeStruct((N_DIRS, n, SSM_W), F32),
            jax.ShapeDtypeStruct((n_seq, N_DIRS, N_SSM_HEADS, SSM_STATE, SSM_HEAD_DIM), F32),
        ],
        compiler_params=_cparams(("arbitrary",)),
        name="ssd_scan",
    )(sched, xbc_act, dt, dtT, dt_bias, dt_biasT, a_coef, a_coefT, d_skip_l, h0T)


def _route_slots(scores, bias):
    tm = scores.shape[1]
    per_group = N_EXPERTS // N_EXPERT_GROUPS
    sel = scores + bias
    io_g = lax.broadcasted_iota(jnp.int32, (per_group, tm), 0)
    gscore = []
    for g in range(N_EXPERT_GROUPS):
        xg = sel[g * per_group:(g + 1) * per_group]
        m1 = jnp.max(xg, axis=0, keepdims=True)
        first = jnp.min(jnp.where(xg == m1, io_g, per_group), axis=0, keepdims=True)
        m2 = jnp.max(jnp.where(io_g == first, -jnp.inf, xg), axis=0, keepdims=True)
        gscore.append(m1 + m2)
    masked = []
    for g in range(N_EXPERT_GROUPS):
        beat = jnp.zeros((1, tm), jnp.int32)
        for o in range(N_EXPERT_GROUPS):
            if o == g:
                continue
            wins = (gscore[o] >= gscore[g]) if o < g else (gscore[o] > gscore[g])
            beat = beat + wins.astype(jnp.int32)
        masked.append(jnp.where(beat < TOPK_GROUPS, sel[g * per_group:(g + 1) * per_group], -jnp.inf))
    selm = jnp.concatenate(masked, axis=0)
    io_e = lax.broadcasted_iota(jnp.int32, (N_EXPERTS, tm), 0)
    rank = jnp.zeros((N_EXPERTS, tm), jnp.int32)
    for o in range(N_EXPERTS):
        row = selm[o:o + 1]
        rank = rank + jnp.where(row > selm, 1, jnp.where(row == selm, (io_e > o).astype(jnp.int32), 0))
    chosen = rank < TOP_K
    denom = jnp.sum(jnp.where(chosen, scores, 0.0), axis=0, keepdims=True)
    wdense = scores / denom * ROUTED_SCALE
    io_k = lax.broadcasted_iota(jnp.int32, (TOP_K, tm), 0)
    ids = jnp.zeros((TOP_K, tm), F32)
    wts = jnp.zeros((TOP_K, tm), F32)
    io_ef = io_e.astype(F32)
    for k in range(TOP_K):
        hit = rank == k
        ids = jnp.where(io_k == k, jnp.sum(jnp.where(hit, io_ef, 0.0), axis=0, keepdims=True), ids)
        wts = jnp.where(io_k == k, jnp.sum(jnp.where(hit, wdense, 0.0), axis=0, keepdims=True), wts)
    return ids.astype(jnp.int32), wts


def _outproj_kernel(x_ref, pool_ref, attc_ref, attl_ref, y_ref, z_ref, sg_ref, w_ref, mod_ref, g2_ref, wr_ref, rb_ref,
                    x1_ref, h2_ref, eid_ref, ew_ref, *, n_ctx_tiles):
    att = jnp.where(pl.program_id(0) < n_ctx_tiles, attc_ref[...], attl_ref[...])
    y = (y_ref[0] + y_ref[1]) * _silu(z_ref[...])
    half = SSM_W // SSM_GROUPS
    parts = []
    for g in range(SSM_GROUPS):
        yg = y[:, g * half:(g + 1) * half]
        parts.append(yg * lax.rsqrt(jnp.mean(yg * yg, axis=-1, keepdims=True) + EPS))
    ssm = jnp.concatenate(parts, axis=-1) * sg_ref[...]
    cat = jnp.concatenate([pool_ref[...], att, ssm], axis=-1).astype(BF16)
    mixed = jnp.dot(cat, w_ref[...], preferred_element_type=F32)
    x1 = x_ref[...] + mod_ref[0, 2:3, :] * mixed
    x1_ref[...] = x1
    h2 = x1 * lax.rsqrt(jnp.mean(x1 * x1, axis=-1, keepdims=True) + EPS) * g2_ref[...]
    h2 = h2 * (1.0 + mod_ref[0, 4:5, :]) + mod_ref[0, 3:4, :]
    for j in range(D_MODEL // LANES):
        h2_ref[pl.ds(j, TM, stride=SUBLANES), :] = h2[:, j * LANES:(j + 1) * LANES]
    logits = lax.dot_general(wr_ref[...], h2, (((1,), (1,)), ((), ())), preferred_element_type=F32, precision=HIGHEST)
    ids, wts = _route_slots(jax.nn.sigmoid(logits), rb_ref[...])
    eid_ref[...] = ids
    ew_ref[...] = wts


def _out_projection(x, pool_out, att_ctx, att_lat, y2, z, ssm_g, w_out, mod_l, norm2_g, w_rT, r_bias, n_ctx_tiles,
                    lat_tiles):
    n = x.shape[0]
    nt = n // TM

    def mod_row(i):
        return jnp.where(i < n_ctx_tiles, 0, 1 + (i - n_ctx_tiles) // lat_tiles)

    row = lambda w: pl.BlockSpec((TM, w), lambda i: (i, 0))
    const = lambda a: pl.BlockSpec(a.shape, lambda i: (0,) * a.ndim)
    return pl.pallas_call(
        functools.partial(_outproj_kernel, n_ctx_tiles=n_ctx_tiles),
        grid=(nt,),
        in_specs=[
            row(D_MODEL), row(POOL_W),
            pl.BlockSpec((TM, ATT_W), lambda i: (jnp.minimum(i, n_ctx_tiles - 1), 0)),
            pl.BlockSpec((TM, ATT_W), lambda i: (jnp.maximum(i - n_ctx_tiles, 0), 0)),
            pl.BlockSpec((N_DIRS, TM, SSM_W), lambda i: (0, i, 0)),
            row(SSM_W), const(ssm_g), const(w_out),
            pl.BlockSpec((1, 6, D_MODEL), lambda i: (mod_row(i), 0, 0)),
            const(norm2_g), const(w_rT), const(r_bias),
        ],
        out_specs=[row(D_MODEL), pl.BlockSpec((TM * SUBLANES, LANES), lambda i: (i, 0)),
                   pl.BlockSpec((TOP_K, TM), lambda i: (0, i)), pl.BlockSpec((TOP_K, TM), lambda i: (0, i))],
        out_shape=[
            jax.ShapeDtypeStruct((n, D_MODEL), F32),
            jax.ShapeDtypeStruct((n * SUBLANES, LANES), F32),
            jax.ShapeDtypeStruct((TOP_K, n), jnp.int32),
            jax.ShapeDtypeStruct((TOP_K, n), F32),
        ],
        compiler_params=_cparams(("parallel",)),
        name="out_projection",
    )(x, pool_out, att_ctx, att_lat, y2, z, ssm_g, w_out, mod_l, norm2_g, w_rT, r_bias)


def _moe_kernel(meta_ref, src_ref, dst_ref, wrow_ref, x_ref, wgu_ref, wd_ref, acc_ref, *scr):
    b = pl.program_id(0)
    nb = pl.num_programs(0)
    valid = meta_ref[3 * nb + b] == 1
    first = meta_ref[4 * nb + b] == 1
    C = MOE_CHAIN
    S = MOE_CHAIN + 1
    nch = D_MODEL // LANES
    xg_scrs, y_scrs = scr[:MOE_NCHAIN], scr[MOE_NCHAIN:]

    @pl.when(first)
    def _():
        acc_ref[...] = jnp.zeros_like(acc_ref)

    @pl.when(valid)
    def _():
        ii = lax.broadcasted_iota(jnp.int32, (C, C), 0)
        jj = lax.broadcasted_iota(jnp.int32, (C, C), 1)
        for c in range(MOE_NCHAIN):
            xg, ys = xg_scrs[c], y_scrs[c]
            for r in range(C):
                src = pl.multiple_of(src_ref[0, 0, c * C + r], SUBLANES)
                xg[pl.ds(r, nch, stride=S), :] = x_ref[0, pl.ds(src, SUBLANES), :]
            chunks = [xg[j * S:j * S + C, :] for j in range(nch)]
            xb = jnp.concatenate(chunks, axis=-1).astype(BF16)
            hgu = jnp.dot(xb, wgu_ref[0], preferred_element_type=F32)
            hm = (_silu(hgu[:, :EXPERT_FF]) * hgu[:, EXPERT_FF:]).astype(BF16)
            y = jnp.dot(hm, wd_ref[0], preferred_element_type=F32)
            w_row = jnp.broadcast_to(wrow_ref[0, :, c * C:(c + 1) * C], (C, C))
            w_col = jnp.sum(jnp.where(ii == jj, w_row, 0.0), axis=1, keepdims=True)
            y = y * w_col
            for j in range(nch):
                ys[j * S:j * S + C, :] = y[:, j * LANES:(j + 1) * LANES]
        for c in range(MOE_NCHAIN):
            ys = y_scrs[c]
            for g0 in range(0, C, MOE_RMW_GROUP):
                pend = []
                for r in range(g0, g0 + MOE_RMW_GROUP):
                    dst = pl.multiple_of(dst_ref[0, 0, c * C + r], SUBLANES)
                    pend.append((dst, acc_ref[0, pl.ds(dst, SUBLANES), :] + ys[pl.ds(r, nch, stride=S), :]))
                for dst, val in pend:
                    acc_ref[0, pl.ds(dst, SUBLANES), :] = val


def _routed_experts(meta, src_rows, dst_rows, wrow, x_rows, w_gateup, w_down, n_tiles):
    nb = meta.shape[0] // 5
    R = MOE_BLK
    C = MOE_CHAIN
    rows_in = MOE_TS * SUBLANES
    rows_out = (MOE_TS + SUBLANES) * SUBLANES
    grid_spec = pltpu.PrefetchScalarGridSpec(
        num_scalar_prefetch=1,
        grid=(nb,),
        in_specs=[
            pl.BlockSpec((1, 1, R), lambda b, m: (m[2 * nb + b], 0, 0), memory_space=pltpu.SMEM),
            pl.BlockSpec((1, 1, R), lambda b, m: (m[2 * nb + b], 0, 0), memory_space=pltpu.SMEM),
            pl.BlockSpec((1, 1, R), lambda b, m: (m[2 * nb + b], 0, 0)),
            pl.BlockSpec((1, rows_in, LANES), lambda b, m: (m[b], 0, 0), pipeline_mode=pl.Buffered(1)),
            pl.BlockSpec((1, D_MODEL, 2 * EXPERT_FF), lambda b, m: (m[nb + b], 0, 0)),
            pl.BlockSpec((1, EXPERT_FF, D_MODEL), lambda b, m: (m[nb + b], 0, 0)),
        ],
        out_specs=pl.BlockSpec((1, rows_out, LANES), lambda b, m: (m[b], 0, 0), pipeline_mode=pl.Buffered(1)),
        scratch_shapes=[pltpu.VMEM(((C + 1) * SUBLANES, LANES), F32) for _ in range(2 * MOE_NCHAIN)],
    )
    return pl.pallas_call(
        _moe_kernel,
        grid_spec=grid_spec,
        out_shape=jax.ShapeDtypeStruct((n_tiles, rows_out, LANES), F32),
        compiler_params=_cparams(("arbitrary",)),
        name="routed_experts",
    )(meta, src_rows, dst_rows, wrow, x_rows, w_gateup, w_down)


def _dispatch_plan(eids, ewts):
    n = eids.shape[1]
    s = n * TOP_K
    n_tiles = n // MOE_TS
    nkeys = n_tiles * N_EXPERTS
    tok_bits = (MOE_TS + SUBLANES - 1).bit_length()
    tok = jnp.arange(n, dtype=jnp.int32)
    key = (tok // MOE_TS)[None, :] * N_EXPERTS + eids
    experts = jnp.arange(N_EXPERTS, dtype=jnp.int32)[:, None, None]
    counts = jnp.sum((eids[None] == experts).reshape(N_EXPERTS, TOP_K, n_tiles, MOE_TS), axis=(1, 3), dtype=jnp.int32)
    counts = counts.T.reshape(nkeys)
    padded = (counts + MOE_BLK - 1) // MOE_BLK * MOE_BLK
    j = jnp.arange(MOE_BLK - 1, dtype=jnp.int32)[None, :]
    keys = jnp.arange(nkeys, dtype=jnp.int32)[:, None]
    pad_code = (keys << (tok_bits + 1)) | (1 << tok_bits) | (MOE_TS + j % SUBLANES)
    pad_code = jnp.where(j < (padded - counts)[:, None], pad_code, jnp.iinfo(jnp.int32).max)
    code = (key << (tok_bits + 1)) | (tok % MOE_TS)[None, :]
    n_slots = s + nkeys * (MOE_BLK - 1)
    nb = -(-n_slots // MOE_BLK)
    tail = nb * MOE_BLK - n_slots
    all_code = jnp.concatenate([code.reshape(-1), pad_code.reshape(-1),
                                jnp.full((tail,), jnp.iinfo(jnp.int32).max, jnp.int32)])
    all_w = jnp.concatenate([ewts.reshape(-1), jnp.zeros((nkeys * (MOE_BLK - 1) + tail,), F32)])
    scode, sw = lax.sort((all_code, all_w), num_keys=1)
    stok = scode & ((1 << tok_bits) - 1)
    pend = jnp.cumsum(padded)
    used = pend[-1] // MOE_BLK
    step = jnp.arange(nb, dtype=jnp.int32)
    valid = step < used
    blk = jnp.minimum(step, used - 1)
    k = jnp.searchsorted(pend, blk * MOE_BLK, side="right").astype(jnp.int32)
    btile = k // N_EXPERTS
    bexp = k % N_EXPERTS
    first = valid & jnp.concatenate([jnp.ones((1,), bool), btile[1:] != btile[:-1]])
    meta = jnp.concatenate([btile, bexp, blk, valid.astype(jnp.int32), first.astype(jnp.int32)])
    shp = (nb, 1, MOE_BLK)
    src_rows = jnp.where(stok < MOE_TS, stok, 0) * SUBLANES
    dst_rows = stok * SUBLANES
    return meta.astype(jnp.int32), src_rows.reshape(shp), dst_rows.reshape(shp), sw.reshape(shp)


def _ffn_out_kernel(x1_ref, h2_ref, r_ref, wg_ref, wu_ref, wd_ref, mod_ref, o_ref):
    def token_major(ref2d):
        return jnp.concatenate([ref2d[pl.ds(j, TM, stride=SUBLANES), :] for j in range(D_MODEL // LANES)], axis=-1)

    hb = token_major(h2_ref).astype(BF16)
    routed = token_major(r_ref.at[0])
    hg = jnp.dot(hb, wg_ref[...], preferred_element_type=F32)
    hu = jnp.dot(hb, wu_ref[...], preferred_element_type=F32)
    shared = jnp.dot((_silu(hg) * hu).astype(BF16), wd_ref[...], preferred_element_type=F32)
    o_ref[...] = x1_ref[...] + mod_ref[0, 5:6, :] * (routed + shared)


def _ffn_out(x1, h2, routed, ws_gate, ws_up, ws_down, mod_l, n_ctx_tiles, lat_tiles):
    n = x1.shape[0]
    nt = n // TM
    tiles_per_super = MOE_TS // TM

    def mod_row(i):
        return jnp.where(i < n_ctx_tiles, 0, 1 + (i - n_ctx_tiles) // lat_tiles)

    row = lambda w: pl.BlockSpec((TM, w), lambda i: (i, 0))
    const = lambda a: pl.BlockSpec(a.shape, lambda i: (0,) * a.ndim)
    return pl.pallas_call(
        _ffn_out_kernel,
        grid=(nt,),
        in_specs=[row(D_MODEL), pl.BlockSpec((TM * SUBLANES, LANES), lambda i: (i, 0)),
                  pl.BlockSpec((1, TM * SUBLANES, LANES), lambda i: (i // tiles_per_super, i % tiles_per_super, 0)),
                  const(ws_gate), const(ws_up), const(ws_down),
                  pl.BlockSpec((1, 6, D_MODEL), lambda i: (mod_row(i), 0, 0))],
        out_specs=row(D_MODEL),
        out_shape=jax.ShapeDtypeStruct((n, D_MODEL), F32),
        compiler_params=_cparams(("parallel",)),
        name="ffn_out",
    )(x1, h2, routed, ws_gate, ws_up, ws_down, mod_l)


def _rope_tables(t_seq):
    rows = t_seq // GRID_W
    r, col = jnp.meshgrid(jnp.arange(rows), jnp.arange(GRID_W), indexing="ij")
    pos = jnp.stack([r.reshape(-1), col.reshape(-1)], axis=-1).astype(F32)
    inv = ROPE_THETA ** (-jnp.arange(ROPE_FREQS, dtype=F32) / ROPE_FREQS)
    ang = pos[:, :, None] * inv
    cos = jnp.cos(ang).reshape(t_seq, 2 * ROPE_FREQS)
    sin = jnp.sin(ang).reshape(t_seq, 2 * ROPE_FREQS)
    cos32 = jnp.concatenate([cos, cos], axis=-1)
    sin32 = jnp.concatenate([-sin, sin], axis=-1)
    reps = QK_W // ATT_QK_DIM
    cos_t = jnp.tile(cos32, (1, reps))
    sin_t = jnp.tile(sin32, (1, reps))
    cos_t = jnp.concatenate([cos_t, jnp.ones((TM, QK_W), F32)], axis=0)
    sin_t = jnp.concatenate([sin_t, jnp.zeros((TM, QK_W), F32)], axis=0)
    return cos_t, sin_t


def _block_diag(blocks):
    g, a, b = blocks.shape
    out = jnp.zeros((g * a, g * b), blocks.dtype)
    for i in range(g):
        out = out.at[i * a:(i + 1) * a, i * b:(i + 1) * b].set(blocks[i])
    return out


def _ssd_schedule(n_ctx_seq, ctx_chunks, n_lat_seq, lat_chunks):
    blk, dirs, first, last, seq = [], [], [], [], []
    base = 0
    sid = 0
    for n_seq, nc in ((n_ctx_seq, ctx_chunks), (n_lat_seq, lat_chunks)):
        for _ in range(n_seq):
            for d in range(N_DIRS):
                order = range(nc) if d == 0 else range(nc - 1, -1, -1)
                for k, c in enumerate(order):
                    blk.append(base + c)
                    dirs.append(d)
                    first.append(int(k == 0))
                    last.append(int(k == nc - 1))
                    seq.append(sid)
            base += nc
            sid += 1
    return jnp.asarray(blk + dirs + first + last + seq, jnp.int32)


def kernel(x_prompt, x_sample, cache_k, cache_v, state_ssm, c, c_ctx, w_mod, b_mod, norm1_g, w_in, pool_w, pool_scale, q_norm_g, k_norm_g, lambda_q, lambda_k, subln_g, conv_w, conv_b, dt_bias, a_log, d_skip, ssm_norm_g, w_out, norm2_g, w_router, router_bias, w_gate, w_up, w_down, ws_gate, ws_up, ws_down):
    nb_ctx, t_ctx, d = x_prompt.shape
    nb_lat, t_lat, _ = x_sample.shape
    depth = w_mod.shape[0]
    past = cache_k.shape[2]
    n_ctx = nb_ctx * t_ctx
    n_lat = nb_lat * t_lat
    n = n_ctx + n_lat
    assert d == D_MODEL and t_ctx % TM == 0 and t_lat % TM == 0 and n % MOE_TS == 0
    assert 1 + nb_lat <= SUBLANES
    n_ctx_tiles = n_ctx // TM
    lat_tiles = t_lat // TM
    n_seq = nb_ctx + nb_lat

    x = jnp.concatenate([x_prompt.reshape(n_ctx, d), x_sample.reshape(n_lat, d)], axis=0)
    cond = jnp.zeros((SUBLANES, d), F32).at[0].set(c_ctx).at[1:1 + nb_lat].set(c)
    mod = _modulation(cond, w_mod, b_mod).reshape(depth, SUBLANES, 6, d)

    cos_t, sin_t = _rope_tables(t_lat)
    ones_bd = _block_diag(jnp.ones((QK_W // ATT_QK_DIM, ATT_QK_DIM, ATT_QK_DIM), F32))
    seq_tiles = jnp.asarray([t_ctx // TM] * n_ctx_tiles + [lat_tiles] * (nb_lat * lat_tiles), jnp.int32)
    tile_pos = jnp.asarray([i % (t_ctx // TM) for i in range(n_ctx_tiles)] + [i % lat_tiles for i in range(nb_lat * lat_tiles)], jnp.int32)
    seq_info = jnp.concatenate([seq_tiles, tile_pos])
    sched = _ssd_schedule(nb_ctx, t_ctx // SSM_CHUNK, nb_lat, t_lat // SSM_CHUNK)

    new_k, new_v, new_ssm = [], [], []
    for l in range(depth):
        mod_l = mod[l]
        w_in_p = jnp.pad(w_in[l], ((0, 0), (0, IN_W_PAD - IN_W))).astype(BF16)
        w_dtT = w_in[l][:, IN_W - 16:].T.astype(BF16)
        qg = jnp.tile(q_norm_g[l], QK_W // ATT_QK_DIM)[None]
        kg = jnp.tile(k_norm_g[l], QK_W // ATT_QK_DIM)[None]
        pool_in, qn, kT, kn, v, z, xbc, dt, dtT = _in_projection(
            x, mod_l, norm1_g[l][None], w_in_p, w_dtT, qg, kg, cos_t, sin_t, ones_bd, n_ctx_tiles, lat_tiles)

        wbd = _block_diag(pool_w[l]).astype(BF16)
        pool_out, xbc_act = _token_shift(seq_info, pool_in, xbc, wbd, pool_scale[l][None], conv_w[l], conv_b[l][None])

        lam_init = 0.8 - 0.6 * math.exp(-0.3 * l)
        lq = lambda_q[l].astype(F32)
        lk = lambda_k[l].astype(F32)
        lam = (jnp.exp(jnp.sum(lq[0] * lk[0])) - jnp.exp(jnp.sum(lq[1] * lk[1])) + lam_init).reshape(1)
        vb = v.astype(BF16)
        kT_ctx = kT[:, :n_ctx].reshape(QK_W, nb_ctx, t_ctx).transpose(1, 0, 2)
        v_ctx = vb[:n_ctx].reshape(nb_ctx, t_ctx, N_ATT_HEADS, ATT_V_DIM).transpose(0, 2, 1, 3)
        kT_lat = kT[:, n_ctx:].reshape(QK_W, nb_lat, t_lat).transpose(1, 0, 2)
        ck = cache_k[:, l].reshape(nb_lat, past, QK_W).transpose(0, 2, 1).astype(BF16)
        kT_lat = jnp.concatenate([kT_lat, ck], axis=2)
        v_lat = vb[n_ctx:].reshape(nb_lat, t_lat, N_ATT_HEADS, ATT_V_DIM).transpose(0, 2, 1, 3)
        cv = cache_v[:, l].astype(BF16).transpose(0, 2, 1, 3)
        v_lat = jnp.concatenate([v_lat, cv], axis=2)
        sg = subln_g[l][None]
        att_ctx = _attention(lam, qn, kT_ctx, v_ctx, sg, 0, nb_ctx, t_ctx, lam_init)
        att_lat = _attention(lam, qn, kT_lat, v_lat, sg, n_ctx, nb_lat, t_lat, lam_init)

        h0T = jnp.concatenate([
            jnp.zeros((nb_ctx, N_DIRS, N_SSM_HEADS, SSM_STATE, SSM_HEAD_DIM), F32),
            state_ssm[:, l].astype(F32).transpose(0, 1, 2, 4, 3)], axis=0)
        dtb = dt_bias[l].reshape(1, N_DIRS * N_SSM_HEADS)
        ac = (-jnp.exp(a_log[l].astype(F32))).reshape(1, N_DIRS * N_SSM_HEADS)
        dsk = jnp.repeat(d_skip[l].astype(F32), SSM_HEAD_DIM)[None]
        y2, hfinT = _ssd(sched, xbc_act, dt, dtT, dtb, dtb.T, ac, ac.T, dsk, h0T, n_seq)

        x1, h2, eids, ewts = _out_projection(
            x, pool_out, att_ctx, att_lat, y2, z, ssm_norm_g[l][None], w_out[l].astype(BF16), mod_l, norm2_g[l][None],
            w_router[l].T, router_bias[l].astype(F32)[:, None], n_ctx_tiles, lat_tiles)

        meta, src_rows, dst_rows, w_list = _dispatch_plan(eids, ewts)
        n_tiles = n // MOE_TS
        x_rows = h2.reshape(n_tiles, MOE_TS * SUBLANES, LANES)
        w_gateup = jnp.concatenate([w_gate[l].astype(BF16), w_up[l].astype(BF16)], axis=-1)
        acc = _routed_experts(meta, src_rows, dst_rows, w_list, x_rows, w_gateup, w_down[l].astype(BF16), n_tiles)
        x = _ffn_out(x1, h2, acc, ws_gate[l].astype(BF16), ws_up[l].astype(BF16), ws_down[l].astype(BF16),
                     mod_l, n_ctx_tiles, lat_tiles)

        new_k.append(kn[:n_ctx].reshape(nb_ctx, t_ctx, N_ATT_HEADS, 2, ATT_QK_DIM))
        new_v.append(v[:n_ctx].reshape(nb_ctx, t_ctx, N_ATT_HEADS, ATT_V_DIM))
        new_ssm.append(hfinT[:nb_ctx].transpose(0, 1, 2, 4, 3))

    y_prompt = x[:n_ctx].reshape(nb_ctx, t_ctx, d)
    y_sample = x[n_ctx:].reshape(nb_lat, t_lat, d)
    return (y_prompt, y_sample, jnp.stack(new_k, axis=1), jnp.stack(new_v, axis=1), jnp.stack(new_ssm, axis=1))
```

```python
import functools
import math

import jax
import jax.numpy as jnp
from jax import lax
from jax.experimental import pallas as pl
from jax.experimental.pallas import tpu as pltpu

F32 = jnp.float32
BF16 = jnp.bfloat16

D_MODEL = 1024
GRID_W = 64
POOL_W = 256
POOL_WINDOWS = (2, 4, 8, 16)
POOL_GROUP = 64
N_ATT_HEADS = 4
ATT_V_DIM = 64
ATT_QK_DIM = 32
ATT_W = 256
QK_W = 256
ROPE_FREQS = 8
ROPE_THETA = 10000.0
SSM_W = 512
SSM_HEAD_DIM = 64
N_SSM_HEADS = 8
SSM_GROUPS = 2
SSM_STATE = 64
SSM_CONV = 5
SSM_CHUNK = 128
N_DIRS = 2
CONV_CH = 768
N_EXPERTS = 64
N_EXPERT_GROUPS = 8
TOPK_GROUPS = 4
TOP_K = 8
EXPERT_FF = 256
ROUTED_SCALE = 2.5
EPS = 1e-6
IN_W = 2320
IN_W_PAD = 2432

LANES = 128
SUBLANES = 8
TM = 256
HALO = 8
MOE_TS = 4096
MOE_BLK = 256
MOE_RMW_GROUP = 8
VMEM_LIMIT = 48 * 1024 * 1024

HIGHEST = lax.Precision.HIGHEST


def _cparams(sem):
    return pltpu.CompilerParams(dimension_semantics=sem, vmem_limit_bytes=VMEM_LIMIT)


def _silu(x):
    return x * jax.nn.sigmoid(x)


def _softplus(x):
    return jnp.maximum(x, 0.0) + jnp.log1p(jnp.exp(-jnp.abs(x)))


def _mod_kernel(c_ref, w_ref, b_ref, o_ref):
    c = c_ref[...]
    o_ref[0] = jnp.dot(_silu(c), w_ref[0], preferred_element_type=F32, precision=HIGHEST) + b_ref[0]


def _modulation(cond, w_mod, b_mod):
    L = w_mod.shape[0]
    bn = 1536
    return pl.pallas_call(
        _mod_kernel,
        grid=(L, 6 * D_MODEL // bn),
        in_specs=[
            pl.BlockSpec((SUBLANES, D_MODEL), lambda l, j: (0, 0)),
            pl.BlockSpec((1, D_MODEL, bn), lambda l, j: (l, 0, j)),
            pl.BlockSpec((1, 1, bn), lambda l, j: (l, 0, j)),
        ],
        out_specs=pl.BlockSpec((1, SUBLANES, bn), lambda l, j: (l, 0, j)),
        out_shape=jax.ShapeDtypeStruct((L, SUBLANES, 6 * D_MODEL), F32),
        compiler_params=_cparams(("parallel", "parallel")),
        name="adaln_mod",
    )(cond, w_mod, b_mod.reshape(L, 1, 6 * D_MODEL))


def _inproj_kernel(x_ref, mod_ref, g_ref, w_ref, wdt_ref, qg_ref, kg_ref, cos_ref, sin_ref, ones_ref,
                   pool_ref, qn_ref, kT_ref, kn_ref, v_ref, z_ref, xbc_ref, dt_ref, dtT_ref):
    x = x_ref[...]
    xn = x * lax.rsqrt(jnp.mean(x * x, axis=-1, keepdims=True) + EPS) * g_ref[...]
    h = xn * (1.0 + mod_ref[0, 1:2, :]) + mod_ref[0, 0:1, :]
    hb = h.astype(BF16)
    u = jnp.dot(hb, w_ref[...], preferred_element_type=F32)
    pool_ref[...] = u[:, 0:256]
    v_ref[...] = u[:, 768:1024]
    z_ref[...] = u[:, 1024:1536]
    xbc_ref[...] = u[:, 1536:2304]
    dt_ref[...] = u[:, 2304:2304 + LANES][:, 0:16]
    dtT_ref[...] = lax.dot_general(wdt_ref[...], hb, (((1,), (1,)), ((), ())), preferred_element_type=F32)

    lane = lax.broadcasted_iota(jnp.int32, (TM, QK_W), 1)
    first_half = (lane % ATT_QK_DIM) < (ATT_QK_DIM // 2)
    cos = cos_ref[...]
    sin = sin_ref[...]

    def qk_norm(t, gain):
        ss = jnp.dot(t * t, ones_ref[...], preferred_element_type=F32, precision=HIGHEST)
        return t * lax.rsqrt(ss * (1.0 / ATT_QK_DIM) + EPS) * gain

    def rope(t):
        partner = jnp.where(first_half, pltpu.roll(t, QK_W - ATT_QK_DIM // 2, 1), pltpu.roll(t, ATT_QK_DIM // 2, 1))
        return t * cos + partner * sin

    qn_ref[...] = rope(qk_norm(u[:, 256:512], qg_ref[...])).astype(BF16)
    kn = qk_norm(u[:, 512:768], kg_ref[...])
    kn_ref[...] = kn
    kT_ref[...] = rope(kn).T.astype(BF16)


def _in_projection(x, mod_l, norm_g, w_in_p, w_dtT, qg, kg, cos_t, sin_t, ones_bd, n_ctx_tiles, lat_tiles):
    n = x.shape[0]
    nt = n // TM

    def mod_row(i):
        return jnp.where(i < n_ctx_tiles, 0, 1 + (i - n_ctx_tiles) // lat_tiles)

    def rope_row(i):
        return jnp.where(i < n_ctx_tiles, lat_tiles, (i - n_ctx_tiles) % lat_tiles)

    row = lambda w: pl.BlockSpec((TM, w), lambda i: (i, 0))
    const = lambda a: pl.BlockSpec(a.shape, lambda i: (0,) * a.ndim)
    outs = [
        jax.ShapeDtypeStruct((n, POOL_W), F32),
        jax.ShapeDtypeStruct((n, QK_W), BF16),
        jax.ShapeDtypeStruct((QK_W, n), BF16),
        jax.ShapeDtypeStruct((n, QK_W), F32),
        jax.ShapeDtypeStruct((n, ATT_W), F32),
        jax.ShapeDtypeStruct((n, SSM_W), F32),
        jax.ShapeDtypeStruct((n, CONV_CH), F32),
        jax.ShapeDtypeStruct((n, 16), F32),
        jax.ShapeDtypeStruct((16, n), F32),
    ]
    return pl.pallas_call(
        _inproj_kernel,
        grid=(nt,),
        in_specs=[
            row(D_MODEL),
            pl.BlockSpec((1, 6, D_MODEL), lambda i: (mod_row(i), 0, 0)),
            const(norm_g), const(w_in_p), const(w_dtT), const(qg), const(kg),
            pl.BlockSpec((TM, QK_W), lambda i: (rope_row(i), 0)),
            pl.BlockSpec((TM, QK_W), lambda i: (rope_row(i), 0)),
            const(ones_bd),
        ],
        out_specs=[
            row(POOL_W), row(QK_W), pl.BlockSpec((QK_W, TM), lambda i: (0, i)), row(QK_W), row(ATT_W),
            row(SSM_W), row(CONV_CH), row(16), pl.BlockSpec((16, TM), lambda i: (0, i)),
        ],
        out_shape=outs,
        compiler_params=_cparams(("parallel",)),
        name="in_projection",
    )(x, mod_l, norm_g, w_in_p, w_dtT, qg, kg, cos_t, sin_t, ones_bd)


def _shift_kernel(seq_tiles_ref, pc_ref, pp_ref, pn_ref, xc_ref, xp_ref, xn_ref, wbd_ref, psc_ref, cw_ref, cb_ref,
                  pool_ref, xbc_ref):
    i = pl.program_id(0)
    st = seq_tiles_ref[i]
    pos = seq_tiles_ref[i + pl.num_programs(0)]
    has_prev = pos > 0
    has_next = pos < st - 1
    rows = TM + 2 * HALO
    r = lax.broadcasted_iota(jnp.int32, (rows, 1), 0)
    t = r - HALO + pos * TM
    t_len = st * TM

    def extended(c_ref, p_ref, n_ref):
        return jnp.concatenate([jnp.where(has_prev, p_ref[...], 0.0), c_ref[...],
                                jnp.where(has_next, n_ref[...], 0.0)], axis=0)

    def down(a, s):
        return pltpu.roll(a, s, 0)

    def up(a, s):
        return pltpu.roll(a, rows - s, 0)

    u = extended(pc_ref, pp_ref, pn_ref)
    w2 = u + down(u, 1)
    w4 = down(w2, 1) + up(w2, 1)
    w8 = down(w4, 2) + up(w4, 2)
    w16 = down(w8, 4) + up(w8, 4)
    lane = lax.broadcasted_iota(jnp.int32, (rows, POOL_W), 1)
    grp = lane // POOL_GROUP
    wsum = jnp.where(grp == 0, w2, jnp.where(grp == 1, w4, jnp.where(grp == 2, w8, w16)))
    half = jnp.where(grp == 0, 1, jnp.where(grp == 1, 2, jnp.where(grp == 2, 4, 8)))
    lo = jnp.maximum(t - half, 0)
    hi = jnp.minimum(t + half - 1, t_len - 1)
    cnt = (hi - lo + 1).astype(F32)
    pooled = (wsum / cnt - u)[HALO:HALO + TM]
    mixed = jnp.dot(pooled.astype(BF16), wbd_ref[...], preferred_element_type=F32)
    pool_ref[...] = mixed * psc_ref[...]

    x = extended(xc_ref, xp_ref, xn_ref)
    acc = x * cw_ref[2:3, :] + cb_ref[...]
    acc = acc + down(x, 2) * cw_ref[0:1, :] + down(x, 1) * cw_ref[1:2, :]
    acc = acc + up(x, 1) * cw_ref[3:4, :] + up(x, 2) * cw_ref[4:5, :]
    xbc_ref[...] = _silu(acc[HALO:HALO + TM])


def _token_shift(seq_info, pool_in, xbc, wbd, pool_scale, conv_w, conv_b):
    n = pool_in.shape[0]
    nt = n // TM
    hb = TM // HALO
    nhb = n // HALO

    cur = lambda w: pl.BlockSpec((TM, w), lambda i, s: (i, 0))
    prev = lambda w: pl.BlockSpec((HALO, w), lambda i, s: (jnp.maximum(i * hb - 1, 0), 0))
    nxt = lambda w: pl.BlockSpec((HALO, w), lambda i, s: (jnp.minimum((i + 1) * hb, nhb - 1), 0))
    const = lambda a: pl.BlockSpec(a.shape, lambda i, s: (0,) * a.ndim)
    grid_spec = pltpu.PrefetchScalarGridSpec(
        num_scalar_prefetch=1,
        grid=(nt,),
        in_specs=[cur(POOL_W), prev(POOL_W), nxt(POOL_W), cur(CONV_CH), prev(CONV_CH), nxt(CONV_CH),
                  const(wbd), const(pool_scale), const(conv_w), const(conv_b)],
        out_specs=[cur(POOL_W), cur(CONV_CH)],
    )
    return pl.pallas_call(
        _shift_kernel,
        grid_spec=grid_spec,
        out_shape=[jax.ShapeDtypeStruct((n, POOL_W), F32), jax.ShapeDtypeStruct((n, CONV_CH), F32)],
        compiler_params=_cparams(("parallel",)),
        name="token_shift",
    )(seq_info, pool_in, pool_in, pool_in, xbc, xbc, xbc, wbd, pool_scale, conv_w, conv_b)


def _attn_kernel(lam_ref, q_ref, kT_ref, v_ref, g_ref, o_ref, *, lam_init):
    lam = lam_ref[0]
    c0 = (ATT_QK_DIM ** -0.5) * math.log2(math.e)
    q = q_ref[...]
    outs = []
    for h in range(N_ATT_HEADS):
        parts = []
        for c in range(2):
            off = h * 2 * ATT_QK_DIM + c * ATT_QK_DIM
            s = jnp.dot(q[:, off:off + ATT_QK_DIM], kT_ref[0, off:off + ATT_QK_DIM, :], preferred_element_type=F32)
            m = jnp.max(s, axis=-1, keepdims=True)
            p = jnp.exp2((s - m) * c0)
            pv = jnp.dot(p.astype(BF16), v_ref[0, h], preferred_element_type=F32)
            parts.append(pv * (1.0 / jnp.sum(p, axis=-1, keepdims=True)))
        o = parts[0] - lam * parts[1]
        o = o * lax.rsqrt(jnp.mean(o * o, axis=-1, keepdims=True) + EPS) * g_ref[...] * (1.0 - lam_init)
        outs.append(o)
    o_ref[...] = jnp.concatenate(outs, axis=-1)


def _attention(lam, qn, kT, vh, subln_g, row0, n_seq, t_seq, lam_init):
    s_all = kT.shape[-1]
    tq = TM
    qb = t_seq // tq
    b0 = row0 // tq
    return pl.pallas_call(
        functools.partial(_attn_kernel, lam_init=lam_init),
        grid=(n_seq, qb),
        in_specs=[
            pl.BlockSpec(memory_space=pltpu.SMEM),
            pl.BlockSpec((tq, QK_W), lambda b, i: (b0 + b * qb + i, 0)),
            pl.BlockSpec((1, QK_W, s_all), lambda b, i: (b, 0, 0)),
            pl.BlockSpec((1, N_ATT_HEADS, s_all, ATT_V_DIM), lambda b, i: (b, 0, 0, 0)),
            pl.BlockSpec((1, ATT_V_DIM), lambda b, i: (0, 0)),
        ],
        out_specs=pl.BlockSpec((tq, ATT_W), lambda b, i: (b * qb + i, 0)),
        out_shape=jax.ShapeDtypeStruct((n_seq * t_seq, ATT_W), F32),
        compiler_params=_cparams(("parallel", "parallel")),
        name="diff_attention",
    )(lam, qn, kT, vh, subln_g)


def _ssd_kernel(sched_ref, xbc_ref, dt_ref, dtT_ref, dtb_ref, dtbT_ref, ac_ref, acT_ref, dsk_ref, h0_ref,
                y_ref, hfin_ref, h_scr):
    s = pl.program_id(0)
    ns = pl.num_programs(0)
    fwd = sched_ref[ns + s] == 0
    first = sched_ref[2 * ns + s] == 1
    last = sched_ref[3 * ns + s] == 1
    Q = SSM_CHUNK

    @pl.when(first)
    def _():
        h_scr[...] = h0_ref[0, 0]

    xbc = xbc_ref[...]
    xs = xbc[:, :SSM_W]
    dt = _softplus(dt_ref[...] + dtb_ref[...])
    dtT = _softplus(dtT_ref[...] + dtbT_ref[...])
    a = dt * ac_ref[...]
    aT = dtT * acT_ref[...]
    li = lax.broadcasted_iota(jnp.int32, (Q, Q), 0)
    si = lax.broadcasted_iota(jnp.int32, (Q, Q), 1)
    valid = jnp.where(fwd, li - si, si - li) >= 0
    tri = valid.astype(F32)
    cs = jnp.dot(tri, a, preferred_element_type=F32, precision=HIGHEST)
    csT = lax.dot_general(aT, tri, (((1,), (1,)), ((), ())), preferred_element_type=F32, precision=HIGHEST)
    H = N_SSM_HEADS
    csT = jnp.where(fwd, csT[:H], csT[H:])
    dtTd = jnp.where(fwd, dtT[:H], dtT[H:])
    totT = jnp.dot(jnp.where(fwd, aT[:H], aT[H:]), jnp.ones((Q, Q), F32), preferred_element_type=F32,
                   precision=HIGHEST)
    cs_d = jnp.where(fwd, cs[:, :H], cs[:, H:])
    eye = (lax.broadcasted_iota(jnp.int32, (SSM_STATE, SSM_STATE), 0)
           == lax.broadcasted_iota(jnp.int32, (SSM_STATE, SSM_STATE), 1)).astype(F32)

    ys = []
    per_group = N_SSM_HEADS // SSM_GROUPS
    for g in range(SSM_GROUPS):
        bg = xbc[:, SSM_W + g * SSM_STATE:SSM_W + (g + 1) * SSM_STATE]
        cg = xbc[:, SSM_W + SSM_GROUPS * SSM_STATE + g * SSM_STATE:SSM_W + SSM_GROUPS * SSM_STATE + (g + 1) * SSM_STATE]
        cgb = cg.astype(BF16)
        gram = lax.dot_general(cgb, bg.astype(BF16), (((1,), (1,)), ((), ())), preferred_element_type=F32)
        bgT = lax.dot_general(eye, bg, (((1,), (1,)), ((), ())), preferred_element_type=F32, precision=HIGHEST)
        for hh in range(per_group):
            h = g * per_group + hh
            col = jnp.broadcast_to(cs_d[:, h:h + 1], (Q, Q))
            rowv = csT[h:h + 1]
            dtr = dtTd[h:h + 1]
            totr = totT[h:h + 1]
            decay = jnp.exp(jnp.where(valid, col - rowv, -jnp.inf))
            xh = xs[:, h * SSM_HEAD_DIM:(h + 1) * SSM_HEAD_DIM].astype(BF16)
            y = jnp.dot((gram * decay * dtr).astype(BF16), xh, preferred_element_type=F32)
            hT = h_scr[h]
            y = y + jnp.dot(cgb, hT.astype(BF16), preferred_element_type=F32) * jnp.exp(col[:, :SSM_HEAD_DIM])
            bdT = (bgT * (jnp.exp(totr - rowv) * dtr)).astype(BF16)
            upd = jnp.dot(bdT, xh, preferred_element_type=F32)
            h_scr[h] = jnp.exp(totr[:, :SSM_HEAD_DIM]) * hT + upd
            ys.append(y)
    y = jnp.concatenate(ys, axis=-1)
    y_ref[0] = y + jnp.where(fwd, 1.0, 0.0) * xs * dsk_ref[...]

    @pl.when(last)
    def _():
        hfin_ref[0, 0] = h_scr[...]


def _ssd(sched, xbc_act, dt, dtT, dt_bias, dt_biasT, a_coef, a_coefT, d_skip_l, h0T, n_seq):
    n = xbc_act.shape[0]
    ns = sched.shape[0] // 5
    Q = SSM_CHUNK
    const = lambda a: pl.BlockSpec(a.shape, lambda s, sc: (0,) * a.ndim)
    state_spec = pl.BlockSpec((1, 1, N_SSM_HEADS, SSM_STATE, SSM_HEAD_DIM),
                              lambda s, sc: (sc[4 * ns + s], sc[ns + s], 0, 0, 0))
    grid_spec = pltpu.PrefetchScalarGridSpec(
        num_scalar_prefetch=1,
        grid=(ns,),
        in_specs=[
            pl.BlockSpec((Q, CONV_CH), lambda s, sc: (sc[s], 0)),
            pl.BlockSpec((Q, 16), lambda s, sc: (sc[s], 0)),
            pl.BlockSpec((16, Q), lambda s, sc: (0, sc[s])),
            const(dt_bias), const(dt_biasT), const(a_coef), const(a_coefT), const(d_skip_l),
            state_spec,
        ],
        out_specs=[
            pl.BlockSpec((1, Q, SSM_W), lambda s, sc: (sc[ns + s], sc[s], 0)),
            state_spec,
        ],
        scratch_shapes=[pltpu.VMEM((N_SSM_HEADS, SSM_STATE, SSM_HEAD_DIM), F32)],
    )
    return pl.pallas_call(
        _ssd_kernel,
        grid_spec=grid_spec,
        out_shape=[
            jax.ShapeDtypeStruct((N_DIRS, n, SSM_W), F32),
            jax.ShapeDtypeStruct((n_seq, N_DIRS, N_SSM_HEADS, SSM_STATE, SSM_HEAD_DIM), F32),
        ],
        compiler_params=_cparams(("arbitrary",)),
        name="ssd_scan",
    )(sched, xbc_act, dt, dtT, dt_bias, dt_biasT, a_coef, a_coefT, d_skip_l, h0T)


def _route_slots(scores, bias):
    tm = scores.shape[1]
    per_group = N_EXPERTS // N_EXPERT_GROUPS
    sel = scores + bias
    io_g = lax.broadcasted_iota(jnp.int32, (per_group, tm), 0)
    gscore = []
    for g in range(N_EXPERT_GROUPS):
        xg = sel[g * per_group:(g + 1) * per_group]
        m1 = jnp.max(xg, axis=0, keepdims=True)
        first = jnp.min(jnp.where(xg == m1, io_g, per_group), axis=0, keepdims=True)
        m2 = jnp.max(jnp.where(io_g == first, -jnp.inf, xg), axis=0, keepdims=True)
        gscore.append(m1 + m2)
    masked = []
    for g in range(N_EXPERT_GROUPS):
        beat = jnp.zeros((1, tm), jnp.int32)
        for o in range(N_EXPERT_GROUPS):
            if o == g:
                continue
            wins = (gscore[o] >= gscore[g]) if o < g else (gscore[o] > gscore[g])
            beat = beat + wins.astype(jnp.int32)
        masked.append(jnp.where(beat < TOPK_GROUPS, sel[g * per_group:(g + 1) * per_group], -jnp.inf))
    selm = jnp.concatenate(masked, axis=0)
    io_e = lax.broadcasted_iota(jnp.int32, (N_EXPERTS, tm), 0)
    rank = jnp.zeros((N_EXPERTS, tm), jnp.int32)
    for o in range(N_EXPERTS):
        row = selm[o:o + 1]
        rank = rank + jnp.where(row > selm, 1, jnp.where(row == selm, (io_e > o).astype(jnp.int32), 0))
    chosen = rank < TOP_K
    denom = jnp.sum(jnp.where(chosen, scores, 0.0), axis=0, keepdims=True)
    wdense = scores / denom * ROUTED_SCALE
    io_k = lax.broadcasted_iota(jnp.int32, (TOP_K, tm), 0)
    ids = jnp.zeros((TOP_K, tm), F32)
    wts = jnp.zeros((TOP_K, tm), F32)
    io_ef = io_e.astype(F32)
    for k in range(TOP_K):
        hit = rank == k
        ids = jnp.where(io_k == k, jnp.sum(jnp.where(hit, io_ef, 0.0), axis=0, keepdims=True), ids)
        wts = jnp.where(io_k == k, jnp.sum(jnp.where(hit, wdense, 0.0), axis=0, keepdims=True), wts)
    return ids.astype(jnp.int32), wts


def _outproj_kernel(x_ref, pool_ref, attc_ref, attl_ref, y_ref, z_ref, sg_ref, w_ref, mod_ref, g2_ref, wr_ref, rb_ref,
                    x1_ref, h2_ref, eid_ref, ew_ref, *, n_ctx_tiles):
    att = jnp.where(pl.program_id(0) < n_ctx_tiles, attc_ref[...], attl_ref[...])
    y = (y_ref[0] + y_ref[1]) * _silu(z_ref[...])
    half = SSM_W // SSM_GROUPS
    parts = []
    for g in range(SSM_GROUPS):
        yg = y[:, g * half:(g + 1) * half]
        parts.append(yg * lax.rsqrt(jnp.mean(yg * yg, axis=-1, keepdims=True) + EPS))
    ssm = jnp.concatenate(parts, axis=-1) * sg_ref[...]
    cat = jnp.concatenate([pool_ref[...], att, ssm], axis=-1).astype(BF16)
    mixed = jnp.dot(cat, w_ref[...], preferred_element_type=F32)
    x1 = x_ref[...] + mod_ref[0, 2:3, :] * mixed
    x1_ref[...] = x1
    h2 = x1 * lax.rsqrt(jnp.mean(x1 * x1, axis=-1, keepdims=True) + EPS) * g2_ref[...]
    h2 = h2 * (1.0 + mod_ref[0, 4:5, :]) + mod_ref[0, 3:4, :]
    for j in range(D_MODEL // LANES):
        h2_ref[pl.ds(j, TM, stride=SUBLANES), :] = h2[:, j * LANES:(j + 1) * LANES]
    logits = lax.dot_general(wr_ref[...], h2, (((1,), (1,)), ((), ())), preferred_element_type=F32, precision=HIGHEST)
    ids, wts = _route_slots(jax.nn.sigmoid(logits), rb_ref[...])
    eid_ref[...] = ids
    ew_ref[...] = wts


def _out_projection(x, pool_out, att_ctx, att_lat, y2, z, ssm_g, w_out, mod_l, norm2_g, w_rT, r_bias, n_ctx_tiles,
                    lat_tiles):
    n = x.shape[0]
    nt = n // TM

    def mod_row(i):
        return jnp.where(i < n_ctx_tiles, 0, 1 + (i - n_ctx_tiles) // lat_tiles)

    row = lambda w: pl.BlockSpec((TM, w), lambda i: (i, 0))
    const = lambda a: pl.BlockSpec(a.shape, lambda i: (0,) * a.ndim)
    return pl.pallas_call(
        functools.partial(_outproj_kernel, n_ctx_tiles=n_ctx_tiles),
        grid=(nt,),
        in_specs=[
            row(D_MODEL), row(POOL_W),
            pl.BlockSpec((TM, ATT_W), lambda i: (jnp.minimum(i, n_ctx_tiles - 1), 0)),
            pl.BlockSpec((TM, ATT_W), lambda i: (jnp.maximum(i - n_ctx_tiles, 0), 0)),
            pl.BlockSpec((N_DIRS, TM, SSM_W), lambda i: (0, i, 0)),
            row(SSM_W), const(ssm_g), const(w_out),
            pl.BlockSpec((1, 6, D_MODEL), lambda i: (mod_row(i), 0, 0)),
            const(norm2_g), const(w_rT), const(r_bias),
        ],
        out_specs=[row(D_MODEL), pl.BlockSpec((TM * SUBLANES, LANES), lambda i: (i, 0)),
                   pl.BlockSpec((TOP_K, TM), lambda i: (0, i)), pl.BlockSpec((TOP_K, TM), lambda i: (0, i))],
        out_shape=[
            jax.ShapeDtypeStruct((n, D_MODEL), F32),
            jax.ShapeDtypeStruct((n * SUBLANES, LANES), F32),
            jax.ShapeDtypeStruct((TOP_K, n), jnp.int32),
            jax.ShapeDtypeStruct((TOP_K, n), F32),
        ],
        compiler_params=_cparams(("parallel",)),
        name="out_projection",
    )(x, pool_out, att_ctx, att_lat, y2, z, ssm_g, w_out, mod_l, norm2_g, w_rT, r_bias)


def _moe_kernel(meta_ref, src_ref, dst_ref, pdst_ref, wrow_ref, x_ref, wgu_ref, wd_ref, acc_ref, xg, ys_even, ys_odd):
    b = pl.program_id(0)
    nb = pl.num_programs(0)
    valid = meta_ref[4 * nb + b] == 1
    first = meta_ref[5 * nb + b] == 1
    flush = meta_ref[6 * nb + b] == 1
    even = b % 2 == 0
    R = MOE_BLK
    S = R + 1
    nch = D_MODEL // LANES

    @pl.when(b == 0)
    def _():
        ys_even[...] = jnp.zeros_like(ys_even)
        ys_odd[...] = jnp.zeros_like(ys_odd)

    @pl.when(first)
    def _():
        acc_ref[...] = jnp.zeros_like(acc_ref)

    def scatter_add(ys, rows_ref, after=None):
        for g0 in range(0, R, MOE_RMW_GROUP):
            pend = []
            for r in range(g0, g0 + MOE_RMW_GROUP):
                dst = pl.multiple_of(rows_ref[0, 0, r], SUBLANES)
                row = ys[pl.ds(r, nch, stride=S), :]
                if after is not None:
                    row = row + after
                pend.append((dst, acc_ref[0, pl.ds(dst, SUBLANES), :] + row))
            for dst, val in pend:
                acc_ref[0, pl.ds(dst, SUBLANES), :] = val

    def step(ys_prev, ys_cur):
        for r in range(R):
            src = pl.multiple_of(src_ref[0, 0, r], SUBLANES)
            xg[pl.ds(r, nch, stride=S), :] = x_ref[0, pl.ds(src, SUBLANES), :]
        xb = jnp.concatenate([xg[j * S:j * S + R, :] for j in range(nch)], axis=-1).astype(BF16)
        hgu = jnp.dot(xb, wgu_ref[0], preferred_element_type=F32)
        bits = pltpu.bitcast(hgu[0:SUBLANES, 0:LANES], jnp.uint32)
        zero = pltpu.bitcast(lax.shift_right_logical(bits, jnp.uint32(32)), F32)
        scatter_add(ys_prev, pdst_ref, after=zero)
        hm = (_silu(hgu[:, :EXPERT_FF]) * hgu[:, EXPERT_FF:]).astype(BF16)
        y = jnp.dot(hm, wd_ref[0], preferred_element_type=F32)
        ii = lax.broadcasted_iota(jnp.int32, (R, R), 0)
        jj = lax.broadcasted_iota(jnp.int32, (R, R), 1)
        w_row = jnp.broadcast_to(wrow_ref[0], (R, R))
        w_col = jnp.sum(jnp.where(ii == jj, w_row, 0.0), axis=1, keepdims=True)
        y = y * w_col
        for j in range(nch):
            ys_cur[j * S:j * S + R, :] = y[:, j * LANES:(j + 1) * LANES]

    @pl.when(jnp.logical_and(valid, even))
    def _():
        step(ys_odd, ys_even)

    @pl.when(jnp.logical_and(valid, jnp.logical_not(even)))
    def _():
        step(ys_even, ys_odd)

    @pl.when(jnp.logical_and(flush, even))
    def _():
        scatter_add(ys_even, dst_ref)

    @pl.when(jnp.logical_and(flush, jnp.logical_not(even)))
    def _():
        scatter_add(ys_odd, dst_ref)


def _routed_experts(meta, src_rows, dst_rows, wrow, x_rows, w_gateup, w_down, n_tiles):
    nb = meta.shape[0] // 7
    R = MOE_BLK
    rows_in = MOE_TS * SUBLANES
    rows_out = (MOE_TS + SUBLANES) * SUBLANES
    grid_spec = pltpu.PrefetchScalarGridSpec(
        num_scalar_prefetch=1,
        grid=(nb,),
        in_specs=[
            pl.BlockSpec((1, 1, R), lambda b, m: (m[2 * nb + b], 0, 0), memory_space=pltpu.SMEM),
            pl.BlockSpec((1, 1, R), lambda b, m: (m[2 * nb + b], 0, 0), memory_space=pltpu.SMEM),
            pl.BlockSpec((1, 1, R), lambda b, m: (m[3 * nb + b], 0, 0), memory_space=pltpu.SMEM),
            pl.BlockSpec((1, 1, R), lambda b, m: (m[2 * nb + b], 0, 0)),
            pl.BlockSpec((1, rows_in, LANES), lambda b, m: (m[b], 0, 0), pipeline_mode=pl.Buffered(1)),
            pl.BlockSpec((1, D_MODEL, 2 * EXPERT_FF), lambda b, m: (m[nb + b], 0, 0)),
            pl.BlockSpec((1, EXPERT_FF, D_MODEL), lambda b, m: (m[nb + b], 0, 0)),
        ],
        out_specs=pl.BlockSpec((1, rows_out, LANES), lambda b, m: (m[b], 0, 0), pipeline_mode=pl.Buffered(1)),
        scratch_shapes=[pltpu.VMEM(((R + 1) * SUBLANES, LANES), F32) for _ in range(3)],
    )
    return pl.pallas_call(
        _moe_kernel,
        grid_spec=grid_spec,
        out_shape=jax.ShapeDtypeStruct((n_tiles, rows_out, LANES), F32),
        compiler_params=_cparams(("arbitrary",)),
        name="routed_experts",
    )(meta, src_rows, dst_rows, dst_rows, wrow, x_rows, w_gateup, w_down)


def _dispatch_plan(eids, ewts):
    n = eids.shape[1]
    s = n * TOP_K
    n_tiles = n // MOE_TS
    nkeys = n_tiles * N_EXPERTS
    tok_bits = (MOE_TS + SUBLANES - 1).bit_length()
    tok = jnp.arange(n, dtype=jnp.int32)
    key = (tok // MOE_TS)[None, :] * N_EXPERTS + eids
    experts = jnp.arange(N_EXPERTS, dtype=jnp.int32)[:, None, None]
    counts = jnp.sum((eids[None] == experts).reshape(N_EXPERTS, TOP_K, n_tiles, MOE_TS), axis=(1, 3), dtype=jnp.int32)
    counts = counts.T.reshape(nkeys)
    padded = (counts + MOE_BLK - 1) // MOE_BLK * MOE_BLK
    j = jnp.arange(MOE_BLK - 1, dtype=jnp.int32)[None, :]
    keys = jnp.arange(nkeys, dtype=jnp.int32)[:, None]
    pad_code = (keys << (tok_bits + 1)) | (1 << tok_bits) | (MOE_TS + j % SUBLANES)
    pad_code = jnp.where(j < (padded - counts)[:, None], pad_code, jnp.iinfo(jnp.int32).max)
    code = (key << (tok_bits + 1)) | (tok % MOE_TS)[None, :]
    n_slots = s + nkeys * (MOE_BLK - 1)
    nb = -(-n_slots // MOE_BLK)
    tail = nb * MOE_BLK - n_slots
    all_code = jnp.concatenate([code.reshape(-1), pad_code.reshape(-1),
                                jnp.full((tail,), jnp.iinfo(jnp.int32).max, jnp.int32)])
    all_w = jnp.concatenate([ewts.reshape(-1), jnp.zeros((nkeys * (MOE_BLK - 1) + tail,), F32)])
    scode, sw = lax.sort((all_code, all_w), num_keys=1)
    stok = scode & ((1 << tok_bits) - 1)
    pend = jnp.cumsum(padded)
    used = pend[-1] // MOE_BLK
    step = jnp.arange(nb, dtype=jnp.int32)
    valid = step < used
    blk = jnp.minimum(step, used - 1)
    k = jnp.searchsorted(pend, blk * MOE_BLK, side="right").astype(jnp.int32)
    btile = k // N_EXPERTS
    bexp = k % N_EXPERTS
    changes = btile[1:] != btile[:-1]
    first = valid & jnp.concatenate([jnp.ones((1,), bool), changes])
    flush = valid & jnp.concatenate([changes | ~valid[1:], jnp.ones((1,), bool)])
    pending = jnp.concatenate([jnp.zeros((1,), bool), valid[:-1] & ~flush[:-1]]) & valid
    pblk = jnp.where(pending, jnp.concatenate([blk[:1], blk[:-1]]), nb)
    meta = jnp.concatenate([btile, bexp, blk, pblk, valid.astype(jnp.int32), first.astype(jnp.int32),
                            flush.astype(jnp.int32)])
    src_rows = jnp.where(stok < MOE_TS, stok, 0) * SUBLANES
    spare = (MOE_TS + jnp.arange(MOE_BLK, dtype=jnp.int32) % SUBLANES) * SUBLANES
    dst_rows = jnp.concatenate([stok * SUBLANES, spare])
    return (meta.astype(jnp.int32), src_rows.reshape(nb, 1, MOE_BLK), dst_rows.reshape(nb + 1, 1, MOE_BLK),
            sw.reshape(nb, 1, MOE_BLK))


def _ffn_out_kernel(x1_ref, h2_ref, r_ref, wg_ref, wu_ref, wd_ref, mod_ref, o_ref):
    def token_major(ref2d):
        return jnp.concatenate([ref2d[pl.ds(j, TM, stride=SUBLANES), :] for j in range(D_MODEL // LANES)], axis=-1)

    hb = token_major(h2_ref).astype(BF16)
    routed = token_major(r_ref.at[0])
    hg = jnp.dot(hb, wg_ref[...], preferred_element_type=F32)
    hu = jnp.dot(hb, wu_ref[...], preferred_element_type=F32)
    shared = jnp.dot((_silu(hg) * hu).astype(BF16), wd_ref[...], preferred_element_type=F32)
    o_ref[...] = x1_ref[...] + mod_ref[0, 5:6, :] * (routed + shared)


def _ffn_out(x1, h2, routed, ws_gate, ws_up, ws_down, mod_l, n_ctx_tiles, lat_tiles):
    n = x1.shape[0]
    nt = n // TM
    tiles_per_super = MOE_TS // TM

    def mod_row(i):
        return jnp.where(i < n_ctx_tiles, 0, 1 + (i - n_ctx_tiles) // lat_tiles)

    row = lambda w: pl.BlockSpec((TM, w), lambda i: (i, 0))
    const = lambda a: pl.BlockSpec(a.shape, lambda i: (0,) * a.ndim)
    return pl.pallas_call(
        _ffn_out_kernel,
        grid=(nt,),
        in_specs=[row(D_MODEL), pl.BlockSpec((TM * SUBLANES, LANES), lambda i: (i, 0)),
                  pl.BlockSpec((1, TM * SUBLANES, LANES), lambda i: (i // tiles_per_super, i % tiles_per_super, 0)),
                  const(ws_gate), const(ws_up), const(ws_down),
                  pl.BlockSpec((1, 6, D_MODEL), lambda i: (mod_row(i), 0, 0))],
        out_specs=row(D_MODEL),
        out_shape=jax.ShapeDtypeStruct((n, D_MODEL), F32),
        compiler_params=_cparams(("parallel",)),
        name="ffn_out",
    )(x1, h2, routed, ws_gate, ws_up, ws_down, mod_l)


def _rope_tables(t_seq):
    rows = t_seq // GRID_W
    r, col = jnp.meshgrid(jnp.arange(rows), jnp.arange(GRID_W), indexing="ij")
    pos = jnp.stack([r.reshape(-1), col.reshape(-1)], axis=-1).astype(F32)
    inv = ROPE_THETA ** (-jnp.arange(ROPE_FREQS, dtype=F32) / ROPE_FREQS)
    ang = pos[:, :, None] * inv
    cos = jnp.cos(ang).reshape(t_seq, 2 * ROPE_FREQS)
    sin = jnp.sin(ang).reshape(t_seq, 2 * ROPE_FREQS)
    cos32 = jnp.concatenate([cos, cos], axis=-1)
    sin32 = jnp.concatenate([-sin, sin], axis=-1)
    reps = QK_W // ATT_QK_DIM
    cos_t = jnp.tile(cos32, (1, reps))
    sin_t = jnp.tile(sin32, (1, reps))
    cos_t = jnp.concatenate([cos_t, jnp.ones((TM, QK_W), F32)], axis=0)
    sin_t = jnp.concatenate([sin_t, jnp.zeros((TM, QK_W), F32)], axis=0)
    return cos_t, sin_t


def _block_diag(blocks):
    g, a, b = blocks.shape
    out = jnp.zeros((g * a, g * b), blocks.dtype)
    for i in range(g):
        out = out.at[i * a:(i + 1) * a, i * b:(i + 1) * b].set(blocks[i])
    return out


def _ssd_schedule(n_ctx_seq, ctx_chunks, n_lat_seq, lat_chunks):
    blk, dirs, first, last, seq = [], [], [], [], []
    base = 0
    sid = 0
    for n_seq, nc in ((n_ctx_seq, ctx_chunks), (n_lat_seq, lat_chunks)):
        for _ in range(n_seq):
            for d in range(N_DIRS):
                order = range(nc) if d == 0 else range(nc - 1, -1, -1)
                for k, c in enumerate(order):
                    blk.append(base + c)
                    dirs.append(d)
                    first.append(int(k == 0))
                    last.append(int(k == nc - 1))
                    seq.append(sid)
            base += nc
            sid += 1
    return jnp.asarray(blk + dirs + first + last + seq, jnp.int32)


def kernel(x_prompt, x_sample, cache_k, cache_v, state_ssm, c, c_ctx, w_mod, b_mod, norm1_g, w_in, pool_w, pool_scale, q_norm_g, k_norm_g, lambda_q, lambda_k, subln_g, conv_w, conv_b, dt_bias, a_log, d_skip, ssm_norm_g, w_out, norm2_g, w_router, router_bias, w_gate, w_up, w_down, ws_gate, ws_up, ws_down):
    nb_ctx, t_ctx, d = x_prompt.shape
    nb_lat, t_lat, _ = x_sample.shape
    depth = w_mod.shape[0]
    past = cache_k.shape[2]
    n_ctx = nb_ctx * t_ctx
    n_lat = nb_lat * t_lat
    n = n_ctx + n_lat
    assert d == D_MODEL and t_ctx % TM == 0 and t_lat % TM == 0 and n % MOE_TS == 0
    assert 1 + nb_lat <= SUBLANES
    n_ctx_tiles = n_ctx // TM
    lat_tiles = t_lat // TM
    n_seq = nb_ctx + nb_lat

    x = jnp.concatenate([x_prompt.reshape(n_ctx, d), x_sample.reshape(n_lat, d)], axis=0)
    cond = jnp.zeros((SUBLANES, d), F32).at[0].set(c_ctx).at[1:1 + nb_lat].set(c)
    mod = _modulation(cond, w_mod, b_mod).reshape(depth, SUBLANES, 6, d)

    cos_t, sin_t = _rope_tables(t_lat)
    ones_bd = _block_diag(jnp.ones((QK_W // ATT_QK_DIM, ATT_QK_DIM, ATT_QK_DIM), F32))
    seq_tiles = jnp.asarray([t_ctx // TM] * n_ctx_tiles + [lat_tiles] * (nb_lat * lat_tiles), jnp.int32)
    tile_pos = jnp.asarray([i % (t_ctx // TM) for i in range(n_ctx_tiles)] + [i % lat_tiles for i in range(nb_lat * lat_tiles)], jnp.int32)
    seq_info = jnp.concatenate([seq_tiles, tile_pos])
    sched = _ssd_schedule(nb_ctx, t_ctx // SSM_CHUNK, nb_lat, t_lat // SSM_CHUNK)

    new_k, new_v, new_ssm = [], [], []
    for l in range(depth):
        mod_l = mod[l]
        w_in_p = jnp.pad(w_in[l], ((0, 0), (0, IN_W_PAD - IN_W))).astype(BF16)
        w_dtT = w_in[l][:, IN_W - 16:].T.astype(BF16)
        qg = jnp.tile(q_norm_g[l], QK_W // ATT_QK_DIM)[None]
        kg = jnp.tile(k_norm_g[l], QK_W // ATT_QK_DIM)[None]
        pool_in, qn, kT, kn, v, z, xbc, dt, dtT = _in_projection(
            x, mod_l, norm1_g[l][None], w_in_p, w_dtT, qg, kg, cos_t, sin_t, ones_bd, n_ctx_tiles, lat_tiles)

        wbd = _block_diag(pool_w[l]).astype(BF16)
        pool_out, xbc_act = _token_shift(seq_info, pool_in, xbc, wbd, pool_scale[l][None], conv_w[l], conv_b[l][None])

        lam_init = 0.8 - 0.6 * math.exp(-0.3 * l)
        lq = lambda_q[l].astype(F32)
        lk = lambda_k[l].astype(F32)
        lam = (jnp.exp(jnp.sum(lq[0] * lk[0])) - jnp.exp(jnp.sum(lq[1] * lk[1])) + lam_init).reshape(1)
        vb = v.astype(BF16)
        kT_ctx = kT[:, :n_ctx].reshape(QK_W, nb_ctx, t_ctx).transpose(1, 0, 2)
        v_ctx = vb[:n_ctx].reshape(nb_ctx, t_ctx, N_ATT_HEADS, ATT_V_DIM).transpose(0, 2, 1, 3)
        kT_lat = kT[:, n_ctx:].reshape(QK_W, nb_lat, t_lat).transpose(1, 0, 2)
        ck = cache_k[:, l].reshape(nb_lat, past, QK_W).transpose(0, 2, 1).astype(BF16)
        kT_lat = jnp.concatenate([kT_lat, ck], axis=2)
        v_lat = vb[n_ctx:].reshape(nb_lat, t_lat, N_ATT_HEADS, ATT_V_DIM).transpose(0, 2, 1, 3)
        cv = cache_v[:, l].astype(BF16).transpose(0, 2, 1, 3)
        v_lat = jnp.concatenate([v_lat, cv], axis=2)
        sg = subln_g[l][None]
        att_ctx = _attention(lam, qn, kT_ctx, v_ctx, sg, 0, nb_ctx, t_ctx, lam_init)
        att_lat = _attention(lam, qn, kT_lat, v_lat, sg, n_ctx, nb_lat, t_lat, lam_init)

        h0T = jnp.concatenate([
            jnp.zeros((nb_ctx, N_DIRS, N_SSM_HEADS, SSM_STATE, SSM_HEAD_DIM), F32),
            state_ssm[:, l].astype(F32).transpose(0, 1, 2, 4, 3)], axis=0)
        dtb = dt_bias[l].reshape(1, N_DIRS * N_SSM_HEADS)
        ac = (-jnp.exp(a_log[l].astype(F32))).reshape(1, N_DIRS * N_SSM_HEADS)
        dsk = jnp.repeat(d_skip[l].astype(F32), SSM_HEAD_DIM)[None]
        y2, hfinT = _ssd(sched, xbc_act, dt, dtT, dtb, dtb.T, ac, ac.T, dsk, h0T, n_seq)

        x1, h2, eids, ewts = _out_projection(
            x, pool_out, att_ctx, att_lat, y2, z, ssm_norm_g[l][None], w_out[l].astype(BF16), mod_l, norm2_g[l][None],
            w_router[l].T, router_bias[l].astype(F32)[:, None], n_ctx_tiles, lat_tiles)

        meta, src_rows, dst_rows, w_list = _dispatch_plan(eids, ewts)
        n_tiles = n // MOE_TS
        x_rows = h2.reshape(n_tiles, MOE_TS * SUBLANES, LANES)
        w_gateup = jnp.concatenate([w_gate[l].astype(BF16), w_up[l].astype(BF16)], axis=-1)
        acc = _routed_experts(meta, src_rows, dst_rows, w_list, x_rows, w_gateup, w_down[l].astype(BF16), n_tiles)
        x = _ffn_out(x1, h2, acc, ws_gate[l].astype(BF16), ws_up[l].astype(BF16), ws_down[l].astype(BF16),
                     mod_l, n_ctx_tiles, lat_tiles)

        new_k.append(kn[:n_ctx].reshape(nb_ctx, t_ctx, N_ATT_HEADS, 2, ATT_QK_DIM))
        new_v.append(v[:n_ctx].reshape(nb_ctx, t_ctx, N_ATT_HEADS, ATT_V_DIM))
        new_ssm.append(hfinT[:nb_ctx].transpose(0, 1, 2, 4, 3))

    y_prompt = x[:n_ctx].reshape(nb_ctx, t_ctx, d)
    y_sample = x[n_ctx:].reshape(nb_lat, t_lat, d)
    return (y_prompt, y_sample, jnp.stack(new_k, axis=1), jnp.stack(new_v, axis=1), jnp.stack(new_ssm, axis=1))
```

```python
import functools
import math

import jax
import jax.numpy as jnp
from jax import lax
from jax.experimental import pallas as pl
from jax.experimental.pallas import tpu as pltpu

F32 = jnp.float32
BF16 = jnp.bfloat16

D_MODEL = 1024
GRID_W = 64
POOL_W = 256
POOL_WINDOWS = (2, 4, 8, 16)
POOL_GROUP = 64
N_ATT_HEADS = 4
ATT_V_DIM = 64
ATT_QK_DIM = 32
ATT_W = 256
QK_W = 256
ROPE_FREQS = 8
ROPE_THETA = 10000.0
SSM_W = 512
SSM_HEAD_DIM = 64
N_SSM_HEADS = 8
SSM_GROUPS = 2
SSM_STATE = 64
SSM_CONV = 5
SSM_CHUNK = 128
N_DIRS = 2
CONV_CH = 768
N_EXPERTS = 64
N_EXPERT_GROUPS = 8
TOPK_GROUPS = 4
TOP_K = 8
EXPERT_FF = 256
ROUTED_SCALE = 2.5
EPS = 1e-6
IN_W = 2320
IN_W_PAD = 2432

LANES = 128
SUBLANES = 8
TM = 256
TP = 512
HALO = 8
MOE_TS = 4096
MOE_BLK = 256
MOE_RMW_GROUP = 8
MOE_LIST_STEPS = 8
VMEM_LIMIT = 48 * 1024 * 1024

HIGHEST = lax.Precision.HIGHEST


def _cparams(sem):
    return pltpu.CompilerParams(dimension_semantics=sem, vmem_limit_bytes=VMEM_LIMIT)


def _silu(x):
    return x * jax.nn.sigmoid(x)


def _softplus(x):
    return jnp.maximum(x, 0.0) + jnp.log1p(jnp.exp(-jnp.abs(x)))


def _mod_kernel(c_ref, w_ref, b_ref, o_ref):
    c = c_ref[...]
    o_ref[0] = jnp.dot(_silu(c), w_ref[0], preferred_element_type=F32, precision=HIGHEST) + b_ref[0]


def _modulation(cond, w_mod, b_mod):
    L = w_mod.shape[0]
    bn = 1536
    return pl.pallas_call(
        _mod_kernel,
        grid=(L, 6 * D_MODEL // bn),
        in_specs=[
            pl.BlockSpec((SUBLANES, D_MODEL), lambda l, j: (0, 0)),
            pl.BlockSpec((1, D_MODEL, bn), lambda l, j: (l, 0, j)),
            pl.BlockSpec((1, 1, bn), lambda l, j: (l, 0, j)),
        ],
        out_specs=pl.BlockSpec((1, SUBLANES, bn), lambda l, j: (l, 0, j)),
        out_shape=jax.ShapeDtypeStruct((L, SUBLANES, 6 * D_MODEL), F32),
        compiler_params=_cparams(("parallel", "parallel")),
        name="adaln_mod",
    )(cond, w_mod, b_mod.reshape(L, 1, 6 * D_MODEL))


def _inproj_kernel(x_ref, mod_ref, g_ref, w_ref, wdt_ref, qg_ref, kg_ref, cos_ref, sin_ref, ones_ref,
                   pool_ref, qn_ref, kT_ref, kn_ref, v_ref, z_ref, xbc_ref, dt_ref, dtT_ref):
    x = x_ref[...]
    xn = x * lax.rsqrt(jnp.mean(x * x, axis=-1, keepdims=True) + EPS) * g_ref[...]
    h = xn * (1.0 + mod_ref[0, 1:2, :]) + mod_ref[0, 0:1, :]
    hb = h.astype(BF16)
    u = jnp.dot(hb, w_ref[...], preferred_element_type=F32)
    pool_ref[...] = u[:, 0:256]
    v_ref[...] = u[:, 768:1024]
    z_ref[...] = u[:, 1024:1536]
    xbc_ref[...] = u[:, 1536:2304]
    dt_ref[...] = u[:, 2304:2304 + LANES][:, 0:16]
    dtT_ref[...] = lax.dot_general(wdt_ref[...], hb, (((1,), (1,)), ((), ())), preferred_element_type=F32)

    lane = lax.broadcasted_iota(jnp.int32, (TP, QK_W), 1)
    first_half = (lane % ATT_QK_DIM) < (ATT_QK_DIM // 2)
    cos = cos_ref[...]
    sin = sin_ref[...]

    def qk_norm(t, gain):
        ss = jnp.dot(t * t, ones_ref[...], preferred_element_type=F32, precision=HIGHEST)
        return t * lax.rsqrt(ss * (1.0 / ATT_QK_DIM) + EPS) * gain

    def rope(t):
        partner = jnp.where(first_half, pltpu.roll(t, QK_W - ATT_QK_DIM // 2, 1), pltpu.roll(t, ATT_QK_DIM // 2, 1))
        return t * cos + partner * sin

    qn_ref[...] = rope(qk_norm(u[:, 256:512], qg_ref[...])).astype(BF16)
    kn = qk_norm(u[:, 512:768], kg_ref[...])
    kn_ref[...] = kn
    kT_ref[...] = rope(kn).T.astype(BF16)


def _in_projection(x, mod_l, norm_g, w_in_p, w_dtT, qg, kg, cos_t, sin_t, ones_bd, n_ctx_tiles, lat_tiles):
    n = x.shape[0]
    nt = n // TP

    def mod_row(i):
        return jnp.where(i < n_ctx_tiles, 0, 1 + (i - n_ctx_tiles) // lat_tiles)

    def rope_row(i):
        return jnp.where(i < n_ctx_tiles, lat_tiles, (i - n_ctx_tiles) % lat_tiles)

    row = lambda w: pl.BlockSpec((TP, w), lambda i: (i, 0))
    const = lambda a: pl.BlockSpec(a.shape, lambda i: (0,) * a.ndim)
    outs = [
        jax.ShapeDtypeStruct((n, POOL_W), F32),
        jax.ShapeDtypeStruct((n, QK_W), BF16),
        jax.ShapeDtypeStruct((QK_W, n), BF16),
        jax.ShapeDtypeStruct((n, QK_W), F32),
        jax.ShapeDtypeStruct((n, ATT_W), F32),
        jax.ShapeDtypeStruct((n, SSM_W), F32),
        jax.ShapeDtypeStruct((n, CONV_CH), F32),
        jax.ShapeDtypeStruct((n, 16), F32),
        jax.ShapeDtypeStruct((16, n), F32),
    ]
    return pl.pallas_call(
        _inproj_kernel,
        grid=(nt,),
        in_specs=[
            row(D_MODEL),
            pl.BlockSpec((1, 6, D_MODEL), lambda i: (mod_row(i), 0, 0)),
            const(norm_g), const(w_in_p), const(w_dtT), const(qg), const(kg),
            pl.BlockSpec((TP, QK_W), lambda i: (rope_row(i), 0)),
            pl.BlockSpec((TP, QK_W), lambda i: (rope_row(i), 0)),
            const(ones_bd),
        ],
        out_specs=[
            row(POOL_W), row(QK_W), pl.BlockSpec((QK_W, TP), lambda i: (0, i)), row(QK_W), row(ATT_W),
            row(SSM_W), row(CONV_CH), row(16), pl.BlockSpec((16, TP), lambda i: (0, i)),
        ],
        out_shape=outs,
        compiler_params=_cparams(("parallel",)),
        name="in_projection",
    )(x, mod_l, norm_g, w_in_p, w_dtT, qg, kg, cos_t, sin_t, ones_bd)


def _shift_kernel(seq_tiles_ref, pc_ref, pp_ref, pn_ref, xc_ref, xp_ref, xn_ref, wbd_ref, psc_ref, cw_ref, cb_ref,
                  pool_ref, xbc_ref):
    i = pl.program_id(0)
    st = seq_tiles_ref[i]
    pos = seq_tiles_ref[i + pl.num_programs(0)]
    has_prev = pos > 0
    has_next = pos < st - 1
    rows = TM + 2 * HALO
    r = lax.broadcasted_iota(jnp.int32, (rows, 1), 0)
    t = r - HALO + pos * TM
    t_len = st * TM

    def extended(c_ref, p_ref, n_ref):
        return jnp.concatenate([jnp.where(has_prev, p_ref[...], 0.0), c_ref[...],
                                jnp.where(has_next, n_ref[...], 0.0)], axis=0)

    def down(a, s):
        return pltpu.roll(a, s, 0)

    def up(a, s):
        return pltpu.roll(a, rows - s, 0)

    u = extended(pc_ref, pp_ref, pn_ref)
    w2 = u + down(u, 1)
    w4 = down(w2, 1) + up(w2, 1)
    w8 = down(w4, 2) + up(w4, 2)
    w16 = down(w8, 4) + up(w8, 4)
    lane = lax.broadcasted_iota(jnp.int32, (rows, POOL_W), 1)
    grp = lane // POOL_GROUP
    wsum = jnp.where(grp == 0, w2, jnp.where(grp == 1, w4, jnp.where(grp == 2, w8, w16)))
    half = jnp.where(grp == 0, 1, jnp.where(grp == 1, 2, jnp.where(grp == 2, 4, 8)))
    lo = jnp.maximum(t - half, 0)
    hi = jnp.minimum(t + half - 1, t_len - 1)
    cnt = (hi - lo + 1).astype(F32)
    pooled = (wsum / cnt - u)[HALO:HALO + TM]
    mixed = jnp.dot(pooled.astype(BF16), wbd_ref[...], preferred_element_type=F32)
    pool_ref[...] = mixed * psc_ref[...]

    x = extended(xc_ref, xp_ref, xn_ref)
    acc = x * cw_ref[2:3, :] + cb_ref[...]
    acc = acc + down(x, 2) * cw_ref[0:1, :] + down(x, 1) * cw_ref[1:2, :]
    acc = acc + up(x, 1) * cw_ref[3:4, :] + up(x, 2) * cw_ref[4:5, :]
    xbc_ref[...] = _silu(acc[HALO:HALO + TM])


def _token_shift(seq_info, pool_in, xbc, wbd, pool_scale, conv_w, conv_b):
    n = pool_in.shape[0]
    nt = n // TM
    hb = TM // HALO
    nhb = n // HALO

    cur = lambda w: pl.BlockSpec((TM, w), lambda i, s: (i, 0))
    prev = lambda w: pl.BlockSpec((HALO, w), lambda i, s: (jnp.maximum(i * hb - 1, 0), 0))
    nxt = lambda w: pl.BlockSpec((HALO, w), lambda i, s: (jnp.minimum((i + 1) * hb, nhb - 1), 0))
    const = lambda a: pl.BlockSpec(a.shape, lambda i, s: (0,) * a.ndim)
    grid_spec = pltpu.PrefetchScalarGridSpec(
        num_scalar_prefetch=1,
        grid=(nt,),
        in_specs=[cur(POOL_W), prev(POOL_W), nxt(POOL_W), cur(CONV_CH), prev(CONV_CH), nxt(CONV_CH),
                  const(wbd), const(pool_scale), const(conv_w), const(conv_b)],
        out_specs=[cur(POOL_W), cur(CONV_CH)],
    )
    return pl.pallas_call(
        _shift_kernel,
        grid_spec=grid_spec,
        out_shape=[jax.ShapeDtypeStruct((n, POOL_W), F32), jax.ShapeDtypeStruct((n, CONV_CH), F32)],
        compiler_params=_cparams(("parallel",)),
        name="token_shift",
    )(seq_info, pool_in, pool_in, pool_in, xbc, xbc, xbc, wbd, pool_scale, conv_w, conv_b)


def _attn_kernel(lam_ref, q_ref, kT_ref, v_ref, g_ref, o_ref, *, lam_init):
    lam = lam_ref[0]
    c0 = (ATT_QK_DIM ** -0.5) * math.log2(math.e)
    q = q_ref[...]
    outs = []
    for h in range(N_ATT_HEADS):
        parts = []
        for c in range(2):
            off = h * 2 * ATT_QK_DIM + c * ATT_QK_DIM
            s = jnp.dot(q[:, off:off + ATT_QK_DIM], kT_ref[0, off:off + ATT_QK_DIM, :], preferred_element_type=F32)
            m = jnp.max(s, axis=-1, keepdims=True)
            p = jnp.exp2((s - m) * c0)
            pv = jnp.dot(p.astype(BF16), v_ref[0, h], preferred_element_type=F32)
            parts.append(pv * (1.0 / jnp.sum(p, axis=-1, keepdims=True)))
        o = parts[0] - lam * parts[1]
        o = o * lax.rsqrt(jnp.mean(o * o, axis=-1, keepdims=True) + EPS) * g_ref[...] * (1.0 - lam_init)
        outs.append(o)
    o_ref[...] = jnp.concatenate(outs, axis=-1)


def _attention(lam, qn, kT, vh, subln_g, row0, n_seq, t_seq, lam_init):
    s_all = kT.shape[-1]
    tq = TM
    qb = t_seq // tq
    b0 = row0 // tq
    return pl.pallas_call(
        functools.partial(_attn_kernel, lam_init=lam_init),
        grid=(n_seq, qb),
        in_specs=[
            pl.BlockSpec(memory_space=pltpu.SMEM),
            pl.BlockSpec((tq, QK_W), lambda b, i: (b0 + b * qb + i, 0)),
            pl.BlockSpec((1, QK_W, s_all), lambda b, i: (b, 0, 0)),
            pl.BlockSpec((1, N_ATT_HEADS, s_all, ATT_V_DIM), lambda b, i: (b, 0, 0, 0)),
            pl.BlockSpec((1, ATT_V_DIM), lambda b, i: (0, 0)),
        ],
        out_specs=pl.BlockSpec((tq, ATT_W), lambda b, i: (b * qb + i, 0)),
        out_shape=jax.ShapeDtypeStruct((n_seq * t_seq, ATT_W), F32),
        compiler_params=_cparams(("parallel", "parallel")),
        name="diff_attention",
    )(lam, qn, kT, vh, subln_g)


def _ssd_kernel(sched_ref, xbc_ref, dt_ref, dtT_ref, dtb_ref, dtbT_ref, ac_ref, acT_ref, dsk_ref, h0_ref,
                y_ref, hfin_ref, h_scr):
    s = pl.program_id(0)
    ns = pl.num_programs(0)
    fwd = sched_ref[ns + s] == 0
    first = sched_ref[2 * ns + s] == 1
    last = sched_ref[3 * ns + s] == 1
    Q = SSM_CHUNK

    @pl.when(first)
    def _():
        h_scr[...] = h0_ref[0, 0]

    xbc = xbc_ref[...]
    xs = xbc[:, :SSM_W]
    dt = _softplus(dt_ref[...] + dtb_ref[...])
    dtT = _softplus(dtT_ref[...] + dtbT_ref[...])
    a = dt * ac_ref[...]
    aT = dtT * acT_ref[...]
    li = lax.broadcasted_iota(jnp.int32, (Q, Q), 0)
    si = lax.broadcasted_iota(jnp.int32, (Q, Q), 1)
    valid = jnp.where(fwd, li - si, si - li) >= 0
    tri = valid.astype(F32)
    cs = jnp.dot(tri, a, preferred_element_type=F32, precision=HIGHEST)
    csT = lax.dot_general(aT, tri, (((1,), (1,)), ((), ())), preferred_element_type=F32, precision=HIGHEST)
    H = N_SSM_HEADS
    csT = jnp.where(fwd, csT[:H], csT[H:])
    dtTd = jnp.where(fwd, dtT[:H], dtT[H:])
    totT = jnp.dot(jnp.where(fwd, aT[:H], aT[H:]), jnp.ones((Q, Q), F32), preferred_element_type=F32,
                   precision=HIGHEST)
    cs_d = jnp.where(fwd, cs[:, :H], cs[:, H:])
    eye = (lax.broadcasted_iota(jnp.int32, (SSM_STATE, SSM_STATE), 0)
           == lax.broadcasted_iota(jnp.int32, (SSM_STATE, SSM_STATE), 1)).astype(F32)

    ys = []
    per_group = N_SSM_HEADS // SSM_GROUPS
    for g in range(SSM_GROUPS):
        bg = xbc[:, SSM_W + g * SSM_STATE:SSM_W + (g + 1) * SSM_STATE]
        cg = xbc[:, SSM_W + SSM_GROUPS * SSM_STATE + g * SSM_STATE:SSM_W + SSM_GROUPS * SSM_STATE + (g + 1) * SSM_STATE]
        cgb = cg.astype(BF16)
        gram = lax.dot_general(cgb, bg.astype(BF16), (((1,), (1,)), ((), ())), preferred_element_type=F32)
        bgT = lax.dot_general(eye, bg, (((1,), (1,)), ((), ())), preferred_element_type=F32, precision=HIGHEST)
        for hh in range(per_group):
            h = g * per_group + hh
            col = jnp.broadcast_to(cs_d[:, h:h + 1], (Q, Q))
            rowv = csT[h:h + 1]
            dtr = dtTd[h:h + 1]
            totr = totT[h:h + 1]
            decay = jnp.exp(jnp.where(valid, col - rowv, -jnp.inf))
            xh = xs[:, h * SSM_HEAD_DIM:(h + 1) * SSM_HEAD_DIM].astype(BF16)
            y = jnp.dot((gram * decay * dtr).astype(BF16), xh, preferred_element_type=F32)
            hT = h_scr[h]
            y = y + jnp.dot(cgb, hT.astype(BF16), preferred_element_type=F32) * jnp.exp(col[:, :SSM_HEAD_DIM])
            bdT = (bgT * (jnp.exp(totr - rowv) * dtr)).astype(BF16)
            upd = jnp.dot(bdT, xh, preferred_element_type=F32)
            h_scr[h] = jnp.exp(totr[:, :SSM_HEAD_DIM]) * hT + upd
            ys.append(y)
    y = jnp.concatenate(ys, axis=-1)
    y_ref[0] = y + jnp.where(fwd, 1.0, 0.0) * xs * dsk_ref[...]

    @pl.when(last)
    def _():
        hfin_ref[0, 0] = h_scr[...]


def _ssd(sched, xbc_act, dt, dtT, dt_bias, dt_biasT, a_coef, a_coefT, d_skip_l, h0T, n_seq):
    n = xbc_act.shape[0]
    ns = sched.shape[0] // 5
    Q = SSM_CHUNK
    const = lambda a: pl.BlockSpec(a.shape, lambda s, sc: (0,) * a.ndim)
    state_spec = pl.BlockSpec((1, 1, N_SSM_HEADS, SSM_STATE, SSM_HEAD_DIM),
                              lambda s, sc: (sc[4 * ns + s], sc[ns + s], 0, 0, 0))
    grid_spec = pltpu.PrefetchScalarGridSpec(
        num_scalar_prefetch=1,
        grid=(ns,),
        in_specs=[
            pl.BlockSpec((Q, CONV_CH), lambda s, sc: (sc[s], 0)),
            pl.BlockSpec((Q, 16), lambda s, sc: (sc[s], 0)),
            pl.BlockSpec((16, Q), lambda s, sc: (0, sc[s])),
            const(dt_bias), const(dt_biasT), const(a_coef), const(a_coefT), const(d_skip_l),
            state_spec,
        ],
        out_specs=[
            pl.BlockSpec((1, Q, SSM_W), lambda s, sc: (sc[ns + s], sc[s], 0)),
            state_spec,
        ],
        scratch_shapes=[pltpu.VMEM((N_SSM_HEADS, SSM_STATE, SSM_HEAD_DIM), F32)],
    )
    return pl.pallas_call(
        _ssd_kernel,
        grid_spec=grid_spec,
        out_shape=[
            jax.ShapeDtypeStruct((N_DIRS, n, SSM_W), F32),
            jax.ShapeDtypeStruct((n_seq, N_DIRS, N_SSM_HEADS, SSM_STATE, SSM_HEAD_DIM), F32),
        ],
        compiler_params=_cparams(("arbitrary",)),
        name="ssd_scan",
    )(sched, xbc_act, dt, dtT, dt_bias, dt_biasT, a_coef, a_coefT, d_skip_l, h0T)


def _route_slots(scores, bias):
    tm = scores.shape[1]
    per_group = N_EXPERTS // N_EXPERT_GROUPS
    sel = scores + bias
    io_g = lax.broadcasted_iota(jnp.int32, (per_group, tm), 0)
    gscore = []
    for g in range(N_EXPERT_GROUPS):
        xg = sel[g * per_group:(g + 1) * per_group]
        m1 = jnp.max(xg, axis=0, keepdims=True)
        first = jnp.min(jnp.where(xg == m1, io_g, per_group), axis=0, keepdims=True)
        m2 = jnp.max(jnp.where(io_g == first, -jnp.inf, xg), axis=0, keepdims=True)
        gscore.append(m1 + m2)
    masked = []
    for g in range(N_EXPERT_GROUPS):
        beat = jnp.zeros((1, tm), jnp.int32)
        for o in range(N_EXPERT_GROUPS):
            if o == g:
                continue
            wins = (gscore[o] >= gscore[g]) if o < g else (gscore[o] > gscore[g])
            beat = beat + wins.astype(jnp.int32)
        masked.append(jnp.where(beat < TOPK_GROUPS, sel[g * per_group:(g + 1) * per_group], -jnp.inf))
    selm = jnp.concatenate(masked, axis=0)
    io_e = lax.broadcasted_iota(jnp.int32, (N_EXPERTS, tm), 0)
    rank = jnp.zeros((N_EXPERTS, tm), jnp.int32)
    for o in range(N_EXPERTS):
        row = selm[o:o + 1]
        rank = rank + jnp.where(row > selm, 1, jnp.where(row == selm, (io_e > o).astype(jnp.int32), 0))
    chosen = rank < TOP_K
    denom = jnp.sum(jnp.where(chosen, scores, 0.0), axis=0, keepdims=True)
    wdense = scores / denom * ROUTED_SCALE
    io_k = lax.broadcasted_iota(jnp.int32, (TOP_K, tm), 0)
    ids = jnp.zeros((TOP_K, tm), F32)
    wts = jnp.zeros((TOP_K, tm), F32)
    io_ef = io_e.astype(F32)
    for k in range(TOP_K):
        hit = rank == k
        ids = jnp.where(io_k == k, jnp.sum(jnp.where(hit, io_ef, 0.0), axis=0, keepdims=True), ids)
        wts = jnp.where(io_k == k, jnp.sum(jnp.where(hit, wdense, 0.0), axis=0, keepdims=True), wts)
    return ids.astype(jnp.int32), wts


def _outproj_kernel(x_ref, pool_ref, attc_ref, attl_ref, y_ref, z_ref, sg_ref, w_ref, mod_ref, g2_ref, wr_ref, rb_ref,
                    x1_ref, h2_ref, eid_ref, ew_ref, *, n_ctx_tiles):
    att = jnp.where(pl.program_id(0) < n_ctx_tiles, attc_ref[...], attl_ref[...])
    y = (y_ref[0] + y_ref[1]) * _silu(z_ref[...])
    half = SSM_W // SSM_GROUPS
    parts = []
    for g in range(SSM_GROUPS):
        yg = y[:, g * half:(g + 1) * half]
        parts.append(yg * lax.rsqrt(jnp.mean(yg * yg, axis=-1, keepdims=True) + EPS))
    ssm = jnp.concatenate(parts, axis=-1) * sg_ref[...]
    cat = jnp.concatenate([pool_ref[...], att, ssm], axis=-1).astype(BF16)
    mixed = jnp.dot(cat, w_ref[...], preferred_element_type=F32)
    x1 = x_ref[...] + mod_ref[0, 2:3, :] * mixed
    x1_ref[...] = x1
    h2 = x1 * lax.rsqrt(jnp.mean(x1 * x1, axis=-1, keepdims=True) + EPS) * g2_ref[...]
    h2 = h2 * (1.0 + mod_ref[0, 4:5, :]) + mod_ref[0, 3:4, :]
    for j in range(D_MODEL // LANES):
        h2_ref[pl.ds(j, TP, stride=SUBLANES), :] = h2[:, j * LANES:(j + 1) * LANES]
    logits = lax.dot_general(wr_ref[...], h2, (((1,), (1,)), ((), ())), preferred_element_type=F32, precision=HIGHEST)
    ids, wts = _route_slots(jax.nn.sigmoid(logits), rb_ref[...])
    eid_ref[...] = ids
    ew_ref[...] = wts


def _out_projection(x, pool_out, att_ctx, att_lat, y2, z, ssm_g, w_out, mod_l, norm2_g, w_rT, r_bias, n_ctx_tiles,
                    lat_tiles):
    n = x.shape[0]
    nt = n // TP

    def mod_row(i):
        return jnp.where(i < n_ctx_tiles, 0, 1 + (i - n_ctx_tiles) // lat_tiles)

    row = lambda w: pl.BlockSpec((TP, w), lambda i: (i, 0))
    const = lambda a: pl.BlockSpec(a.shape, lambda i: (0,) * a.ndim)
    return pl.pallas_call(
        functools.partial(_outproj_kernel, n_ctx_tiles=n_ctx_tiles),
        grid=(nt,),
        in_specs=[
            row(D_MODEL), row(POOL_W),
            pl.BlockSpec((TP, ATT_W), lambda i: (jnp.minimum(i, n_ctx_tiles - 1), 0)),
            pl.BlockSpec((TP, ATT_W), lambda i: (jnp.maximum(i - n_ctx_tiles, 0), 0)),
            pl.BlockSpec((N_DIRS, TP, SSM_W), lambda i: (0, i, 0)),
            row(SSM_W), const(ssm_g), const(w_out),
            pl.BlockSpec((1, 6, D_MODEL), lambda i: (mod_row(i), 0, 0)),
            const(norm2_g), const(w_rT), const(r_bias),
        ],
        out_specs=[row(D_MODEL), pl.BlockSpec((TP * SUBLANES, LANES), lambda i: (i, 0)),
                   pl.BlockSpec((TOP_K, TP), lambda i: (0, i)), pl.BlockSpec((TOP_K, TP), lambda i: (0, i))],
        out_shape=[
            jax.ShapeDtypeStruct((n, D_MODEL), F32),
            jax.ShapeDtypeStruct((n * SUBLANES, LANES), F32),
            jax.ShapeDtypeStruct((TOP_K, n), jnp.int32),
            jax.ShapeDtypeStruct((TOP_K, n), F32),
        ],
        compiler_params=_cparams(("parallel",)),
        name="out_projection",
    )(x, pool_out, att_ctx, att_lat, y2, z, ssm_g, w_out, mod_l, norm2_g, w_rT, r_bias)


def _moe_kernel(meta_ref, src_ref, dst_ref, pdst_ref, wrow_ref, x_ref, wgu_ref, wd_ref, acc_ref, xg, ys_even, ys_odd):
    b = pl.program_id(0)
    nb = pl.num_programs(0)
    valid = meta_ref[4 * nb + b] == 1
    first = meta_ref[5 * nb + b] == 1
    flush = meta_ref[6 * nb + b] == 1
    even = b % 2 == 0
    R = MOE_BLK
    sub = meta_ref[2 * nb + b] % MOE_LIST_STEPS
    base = sub * R
    pbase = (meta_ref[3 * nb + b] % MOE_LIST_STEPS) * R
    S = R + 1
    nch = D_MODEL // LANES

    @pl.when(b == 0)
    def _():
        ys_even[...] = jnp.zeros_like(ys_even)
        ys_odd[...] = jnp.zeros_like(ys_odd)

    @pl.when(first)
    def _():
        acc_ref[...] = jnp.zeros_like(acc_ref)

    def scatter_add(ys, rows_ref, off, after=None):
        for g0 in range(0, R, MOE_RMW_GROUP):
            pend = []
            for r in range(g0, g0 + MOE_RMW_GROUP):
                dst = pl.multiple_of(rows_ref[0, 0, off + r], SUBLANES)
                row = ys[pl.ds(r, nch, stride=S), :]
                if after is not None:
                    row = row + after
                pend.append((dst, acc_ref[0, pl.ds(dst, SUBLANES), :] + row))
            for dst, val in pend:
                acc_ref[0, pl.ds(dst, SUBLANES), :] = val

    def step(ys_prev, ys_cur):
        for r in range(R):
            src = pl.multiple_of(src_ref[0, 0, base + r], SUBLANES)
            xg[pl.ds(r, nch, stride=S), :] = x_ref[0, pl.ds(src, SUBLANES), :]
        xb = jnp.concatenate([xg[j * S:j * S + R, :] for j in range(nch)], axis=-1).astype(BF16)
        hgu = jnp.dot(xb, wgu_ref[0], preferred_element_type=F32)
        bits = pltpu.bitcast(hgu[0:SUBLANES, 0:LANES], jnp.uint32)
        zero = pltpu.bitcast(lax.shift_right_logical(bits, jnp.uint32(32)), F32)
        scatter_add(ys_prev, pdst_ref, pbase, after=zero)
        hm = (_silu(hgu[:, :EXPERT_FF]) * hgu[:, EXPERT_FF:]).astype(BF16)
        y = jnp.dot(hm, wd_ref[0], preferred_element_type=F32)
        ii = lax.broadcasted_iota(jnp.int32, (R, R), 0)
        jj = lax.broadcasted_iota(jnp.int32, (R, R), 1)
        w_row = jnp.broadcast_to(wrow_ref[0, pl.ds(sub, 1), :], (R, R))
        w_col = jnp.sum(jnp.where(ii == jj, w_row, 0.0), axis=1, keepdims=True)
        y = y * w_col
        for j in range(nch):
            ys_cur[j * S:j * S + R, :] = y[:, j * LANES:(j + 1) * LANES]

    @pl.when(jnp.logical_and(valid, even))
    def _():
        step(ys_odd, ys_even)

    @pl.when(jnp.logical_and(valid, jnp.logical_not(even)))
    def _():
        step(ys_even, ys_odd)

    @pl.when(jnp.logical_and(flush, even))
    def _():
        scatter_add(ys_even, dst_ref, base)

    @pl.when(jnp.logical_and(flush, jnp.logical_not(even)))
    def _():
        scatter_add(ys_odd, dst_ref, base)


def _routed_experts(meta, src_rows, dst_rows, wrow, x_rows, w_gateup, w_down, n_tiles):
    nb = meta.shape[0] // 7
    R = MOE_BLK
    L = MOE_LIST_STEPS
    rows_in = MOE_TS * SUBLANES
    rows_out = (MOE_TS + SUBLANES) * SUBLANES
    grid_spec = pltpu.PrefetchScalarGridSpec(
        num_scalar_prefetch=1,
        grid=(nb,),
        in_specs=[
            pl.BlockSpec((1, 1, L * R), lambda b, m: (m[2 * nb + b] // L, 0, 0), memory_space=pltpu.SMEM),
            pl.BlockSpec((1, 1, L * R), lambda b, m: (m[2 * nb + b] // L, 0, 0), memory_space=pltpu.SMEM),
            pl.BlockSpec((1, 1, L * R), lambda b, m: (m[3 * nb + b] // L, 0, 0), memory_space=pltpu.SMEM),
            pl.BlockSpec((1, L, R), lambda b, m: (m[2 * nb + b] // L, 0, 0)),
            pl.BlockSpec((1, rows_in, LANES), lambda b, m: (m[b], 0, 0), pipeline_mode=pl.Buffered(1)),
            pl.BlockSpec((1, D_MODEL, 2 * EXPERT_FF), lambda b, m: (m[nb + b], 0, 0)),
            pl.BlockSpec((1, EXPERT_FF, D_MODEL), lambda b, m: (m[nb + b], 0, 0)),
        ],
        out_specs=pl.BlockSpec((1, rows_out, LANES), lambda b, m: (m[b], 0, 0), pipeline_mode=pl.Buffered(1)),
        scratch_shapes=[pltpu.VMEM(((R + 1) * SUBLANES, LANES), F32) for _ in range(3)],
    )
    return pl.pallas_call(
        _moe_kernel,
        grid_spec=grid_spec,
        out_shape=jax.ShapeDtypeStruct((n_tiles, rows_out, LANES), F32),
        compiler_params=_cparams(("arbitrary",)),
        name="routed_experts",
    )(meta, src_rows, dst_rows, dst_rows, wrow, x_rows, w_gateup, w_down)


def _dispatch_plan(eids, ewts):
    n = eids.shape[1]
    s = n * TOP_K
    n_tiles = n // MOE_TS
    nkeys = n_tiles * N_EXPERTS
    tok_bits = (MOE_TS + SUBLANES - 1).bit_length()
    tok = jnp.arange(n, dtype=jnp.int32)
    key = (tok // MOE_TS)[None, :] * N_EXPERTS + eids
    experts = jnp.arange(N_EXPERTS, dtype=jnp.int32)[:, None, None]
    counts = jnp.sum((eids[None] == experts).reshape(N_EXPERTS, TOP_K, n_tiles, MOE_TS), axis=(1, 3), dtype=jnp.int32)
    counts = counts.T.reshape(nkeys)
    padded = (counts + MOE_BLK - 1) // MOE_BLK * MOE_BLK
    j = jnp.arange(MOE_BLK - 1, dtype=jnp.int32)[None, :]
    keys = jnp.arange(nkeys, dtype=jnp.int32)[:, None]
    pad_code = (keys << (tok_bits + 1)) | (1 << tok_bits) | (MOE_TS + j % SUBLANES)
    pad_code = jnp.where(j < (padded - counts)[:, None], pad_code, jnp.iinfo(jnp.int32).max)
    code = (key << (tok_bits + 1)) | (tok % MOE_TS)[None, :]
    n_slots = s + nkeys * (MOE_BLK - 1)
    nb = -(-n_slots // (MOE_BLK * MOE_LIST_STEPS)) * MOE_LIST_STEPS
    tail = nb * MOE_BLK - n_slots
    all_code = jnp.concatenate([code.reshape(-1), pad_code.reshape(-1),
                                jnp.full((tail,), jnp.iinfo(jnp.int32).max, jnp.int32)])
    all_w = jnp.concatenate([ewts.reshape(-1), jnp.zeros((nkeys * (MOE_BLK - 1) + tail,), F32)])
    scode, sw = lax.sort((all_code, all_w), num_keys=1)
    stok = scode & ((1 << tok_bits) - 1)
    pend = jnp.cumsum(padded)
    used = pend[-1] // MOE_BLK
    step = jnp.arange(nb, dtype=jnp.int32)
    valid = step < used
    blk = jnp.minimum(step, used - 1)
    k = jnp.searchsorted(pend, blk * MOE_BLK, side="right").astype(jnp.int32)
    btile = k // N_EXPERTS
    bexp = k % N_EXPERTS
    changes = btile[1:] != btile[:-1]
    first = valid & jnp.concatenate([jnp.ones((1,), bool), changes])
    flush = valid & jnp.concatenate([changes | ~valid[1:], jnp.ones((1,), bool)])
    pending = jnp.concatenate([jnp.zeros((1,), bool), valid[:-1] & ~flush[:-1]]) & valid
    pblk = jnp.where(pending, jnp.concatenate([blk[:1], blk[:-1]]), nb)
    meta = jnp.concatenate([btile, bexp, blk, pblk, valid.astype(jnp.int32), first.astype(jnp.int32),
                            flush.astype(jnp.int32)])
    src_rows = jnp.where(stok < MOE_TS, stok, 0) * SUBLANES
    chunk = MOE_BLK * MOE_LIST_STEPS
    spare = (MOE_TS + jnp.arange(chunk, dtype=jnp.int32) % SUBLANES) * SUBLANES
    dst_rows = jnp.concatenate([stok * SUBLANES, spare])
    nchunks = nb // MOE_LIST_STEPS
    return (meta.astype(jnp.int32), src_rows.reshape(nchunks, 1, chunk), dst_rows.reshape(nchunks + 1, 1, chunk),
            sw.reshape(nchunks, MOE_LIST_STEPS, MOE_BLK))


def _ffn_out_kernel(x1_ref, h2_ref, r_ref, wg_ref, wu_ref, wd_ref, mod_ref, o_ref):
    def token_major(ref2d):
        return jnp.concatenate([ref2d[pl.ds(j, TP, stride=SUBLANES), :] for j in range(D_MODEL // LANES)], axis=-1)

    hb = token_major(h2_ref).astype(BF16)
    routed = token_major(r_ref.at[0])
    hg = jnp.dot(hb, wg_ref[...], preferred_element_type=F32)
    hu = jnp.dot(hb, wu_ref[...], preferred_element_type=F32)
    shared = jnp.dot((_silu(hg) * hu).astype(BF16), wd_ref[...], preferred_element_type=F32)
    o_ref[...] = x1_ref[...] + mod_ref[0, 5:6, :] * (routed + shared)


def _ffn_out(x1, h2, routed, ws_gate, ws_up, ws_down, mod_l, n_ctx_tiles, lat_tiles):
    n = x1.shape[0]
    nt = n // TP
    tiles_per_super = MOE_TS // TP

    def mod_row(i):
        return jnp.where(i < n_ctx_tiles, 0, 1 + (i - n_ctx_tiles) // lat_tiles)

    row = lambda w: pl.BlockSpec((TP, w), lambda i: (i, 0))
    const = lambda a: pl.BlockSpec(a.shape, lambda i: (0,) * a.ndim)
    return pl.pallas_call(
        _ffn_out_kernel,
        grid=(nt,),
        in_specs=[row(D_MODEL), pl.BlockSpec((TP * SUBLANES, LANES), lambda i: (i, 0)),
                  pl.BlockSpec((1, TP * SUBLANES, LANES), lambda i: (i // tiles_per_super, i % tiles_per_super, 0)),
                  const(ws_gate), const(ws_up), const(ws_down),
                  pl.BlockSpec((1, 6, D_MODEL), lambda i: (mod_row(i), 0, 0))],
        out_specs=row(D_MODEL),
        out_shape=jax.ShapeDtypeStruct((n, D_MODEL), F32),
        compiler_params=_cparams(("parallel",)),
        name="ffn_out",
    )(x1, h2, routed, ws_gate, ws_up, ws_down, mod_l)


def _rope_tables(t_seq):
    rows = t_seq // GRID_W
    r, col = jnp.meshgrid(jnp.arange(rows), jnp.arange(GRID_W), indexing="ij")
    pos = jnp.stack([r.reshape(-1), col.reshape(-1)], axis=-1).astype(F32)
    inv = ROPE_THETA ** (-jnp.arange(ROPE_FREQS, dtype=F32) / ROPE_FREQS)
    ang = pos[:, :, None] * inv
    cos = jnp.cos(ang).reshape(t_seq, 2 * ROPE_FREQS)
    sin = jnp.sin(ang).reshape(t_seq, 2 * ROPE_FREQS)
    cos32 = jnp.concatenate([cos, cos], axis=-1)
    sin32 = jnp.concatenate([-sin, sin], axis=-1)
    reps = QK_W // ATT_QK_DIM
    cos_t = jnp.tile(cos32, (1, reps))
    sin_t = jnp.tile(sin32, (1, reps))
    cos_t = jnp.concatenate([cos_t, jnp.ones((TP, QK_W), F32)], axis=0)
    sin_t = jnp.concatenate([sin_t, jnp.zeros((TP, QK_W), F32)], axis=0)
    return cos_t, sin_t


def _block_diag(blocks):
    g, a, b = blocks.shape
    out = jnp.zeros((g * a, g * b), blocks.dtype)
    for i in range(g):
        out = out.at[i * a:(i + 1) * a, i * b:(i + 1) * b].set(blocks[i])
    return out


def _ssd_schedule(n_ctx_seq, ctx_chunks, n_lat_seq, lat_chunks):
    blk, dirs, first, last, seq = [], [], [], [], []
    base = 0
    sid = 0
    for n_seq, nc in ((n_ctx_seq, ctx_chunks), (n_lat_seq, lat_chunks)):
        for _ in range(n_seq):
            for d in range(N_DIRS):
                order = range(nc) if d == 0 else range(nc - 1, -1, -1)
                for k, c in enumerate(order):
                    blk.append(base + c)
                    dirs.append(d)
                    first.append(int(k == 0))
                    last.append(int(k == nc - 1))
                    seq.append(sid)
            base += nc
            sid += 1
    return jnp.asarray(blk + dirs + first + last + seq, jnp.int32)


def kernel(x_prompt, x_sample, cache_k, cache_v, state_ssm, c, c_ctx, w_mod, b_mod, norm1_g, w_in, pool_w, pool_scale, q_norm_g, k_norm_g, lambda_q, lambda_k, subln_g, conv_w, conv_b, dt_bias, a_log, d_skip, ssm_norm_g, w_out, norm2_g, w_router, router_bias, w_gate, w_up, w_down, ws_gate, ws_up, ws_down):
    nb_ctx, t_ctx, d = x_prompt.shape
    nb_lat, t_lat, _ = x_sample.shape
    depth = w_mod.shape[0]
    past = cache_k.shape[2]
    n_ctx = nb_ctx * t_ctx
    n_lat = nb_lat * t_lat
    n = n_ctx + n_lat
    assert d == D_MODEL and t_ctx % TM == 0 and t_lat % TM == 0 and n % MOE_TS == 0
    assert n_ctx % TP == 0 and t_lat % TP == 0 and MOE_TS % TP == 0
    assert 1 + nb_lat <= SUBLANES
    n_ctx_tiles = n_ctx // TM
    lat_tiles = t_lat // TM
    p_ctx_tiles = n_ctx // TP
    p_lat_tiles = t_lat // TP
    n_seq = nb_ctx + nb_lat

    x = jnp.concatenate([x_prompt.reshape(n_ctx, d), x_sample.reshape(n_lat, d)], axis=0)
    cond = jnp.zeros((SUBLANES, d), F32).at[0].set(c_ctx).at[1:1 + nb_lat].set(c)
    mod = _modulation(cond, w_mod, b_mod).reshape(depth, SUBLANES, 6, d)

    cos_t, sin_t = _rope_tables(t_lat)
    ones_bd = _block_diag(jnp.ones((QK_W // ATT_QK_DIM, ATT_QK_DIM, ATT_QK_DIM), F32))
    seq_tiles = jnp.asarray([t_ctx // TM] * n_ctx_tiles + [lat_tiles] * (nb_lat * lat_tiles), jnp.int32)
    tile_pos = jnp.asarray([i % (t_ctx // TM) for i in range(n_ctx_tiles)] + [i % lat_tiles for i in range(nb_lat * lat_tiles)], jnp.int32)
    seq_info = jnp.concatenate([seq_tiles, tile_pos])
    sched = _ssd_schedule(nb_ctx, t_ctx // SSM_CHUNK, nb_lat, t_lat // SSM_CHUNK)

    new_k, new_v, new_ssm = [], [], []
    for l in range(depth):
        mod_l = mod[l]
        w_in_p = jnp.pad(w_in[l], ((0, 0), (0, IN_W_PAD - IN_W))).astype(BF16)
        w_dtT = w_in[l][:, IN_W - 16:].T.astype(BF16)
        qg = jnp.tile(q_norm_g[l], QK_W // ATT_QK_DIM)[None]
        kg = jnp.tile(k_norm_g[l], QK_W // ATT_QK_DIM)[None]
        pool_in, qn, kT, kn, v, z, xbc, dt, dtT = _in_projection(
            x, mod_l, norm1_g[l][None], w_in_p, w_dtT, qg, kg, cos_t, sin_t, ones_bd, p_ctx_tiles, p_lat_tiles)

        wbd = _block_diag(pool_w[l]).astype(BF16)
        pool_out, xbc_act = _token_shift(seq_info, pool_in, xbc, wbd, pool_scale[l][None], conv_w[l], conv_b[l][None])

        lam_init = 0.8 - 0.6 * math.exp(-0.3 * l)
        lq = lambda_q[l].astype(F32)
        lk = lambda_k[l].astype(F32)
        lam = (jnp.exp(jnp.sum(lq[0] * lk[0])) - jnp.exp(jnp.sum(lq[1] * lk[1])) + lam_init).reshape(1)
        vb = v.astype(BF16)
        kT_ctx = kT[:, :n_ctx].reshape(QK_W, nb_ctx, t_ctx).transpose(1, 0, 2)
        v_ctx = vb[:n_ctx].reshape(nb_ctx, t_ctx, N_ATT_HEADS, ATT_V_DIM).transpose(0, 2, 1, 3)
        kT_lat = kT[:, n_ctx:].reshape(QK_W, nb_lat, t_lat).transpose(1, 0, 2)
        ck = cache_k[:, l].reshape(nb_lat, past, QK_W).transpose(0, 2, 1).astype(BF16)
        kT_lat = jnp.concatenate([kT_lat, ck], axis=2)
        v_lat = vb[n_ctx:].reshape(nb_lat, t_lat, N_ATT_HEADS, ATT_V_DIM).transpose(0, 2, 1, 3)
        cv = cache_v[:, l].astype(BF16).transpose(0, 2, 1, 3)
        v_lat = jnp.concatenate([v_lat, cv], axis=2)
        sg = subln_g[l][None]
        att_ctx = _attention(lam, qn, kT_ctx, v_ctx, sg, 0, nb_ctx, t_ctx, lam_init)
        att_lat = _attention(lam, qn, kT_lat, v_lat, sg, n_ctx, nb_lat, t_lat, lam_init)

        h0T = jnp.concatenate([
            jnp.zeros((nb_ctx, N_DIRS, N_SSM_HEADS, SSM_STATE, SSM_HEAD_DIM), F32),
            state_ssm[:, l].astype(F32).transpose(0, 1, 2, 4, 3)], axis=0)
        dtb = dt_bias[l].reshape(1, N_DIRS * N_SSM_HEADS)
        ac = (-jnp.exp(a_log[l].astype(F32))).reshape(1, N_DIRS * N_SSM_HEADS)
        dsk = jnp.repeat(d_skip[l].astype(F32), SSM_HEAD_DIM)[None]
        y2, hfinT = _ssd(sched, xbc_act, dt, dtT, dtb, dtb.T, ac, ac.T, dsk, h0T, n_seq)

        x1, h2, eids, ewts = _out_projection(
            x, pool_out, att_ctx, att_lat, y2, z, ssm_norm_g[l][None], w_out[l].astype(BF16), mod_l, norm2_g[l][None],
            w_router[l].T, router_bias[l].astype(F32)[:, None], p_ctx_tiles, p_lat_tiles)

        meta, src_rows, dst_rows, w_list = _dispatch_plan(eids, ewts)
        n_tiles = n // MOE_TS
        x_rows = h2.reshape(n_tiles, MOE_TS * SUBLANES, LANES)
        w_gateup = jnp.concatenate([w_gate[l].astype(BF16), w_up[l].astype(BF16)], axis=-1)
        acc = _routed_experts(meta, src_rows, dst_rows, w_list, x_rows, w_gateup, w_down[l].astype(BF16), n_tiles)
        x = _ffn_out(x1, h2, acc, ws_gate[l].astype(BF16), ws_up[l].astype(BF16), ws_down[l].astype(BF16),
                     mod_l, p_ctx_tiles, p_lat_tiles)

        new_k.append(kn[:n_ctx].reshape(nb_ctx, t_ctx, N_ATT_HEADS, 2, ATT_QK_DIM))
        new_v.append(v[:n_ctx].reshape(nb_ctx, t_ctx, N_ATT_HEADS, ATT_V_DIM))
        new_ssm.append(hfinT[:nb_ctx].transpose(0, 1, 2, 4, 3))

    y_prompt = x[:n_ctx].reshape(nb_ctx, t_ctx, d)
    y_sample = x[n_ctx:].reshape(nb_lat, t_lat, d)
    return (y_prompt, y_sample, jnp.stack(new_k, axis=1), jnp.stack(new_v, axis=1), jnp.stack(new_ssm, axis=1))
```

```python
import functools
import math

import jax
import jax.numpy as jnp
from jax import lax
from jax.experimental import pallas as pl
from jax.experimental.pallas import tpu as pltpu

F32 = jnp.float32
BF16 = jnp.bfloat16

D_MODEL = 1024
GRID_W = 64
POOL_W = 256
POOL_WINDOWS = (2, 4, 8, 16)
POOL_GROUP = 64
N_ATT_HEADS = 4
ATT_V_DIM = 64
ATT_QK_DIM = 32
ATT_W = 256
QK_W = 256
ROPE_FREQS = 8
ROPE_THETA = 10000.0
SSM_W = 512
SSM_HEAD_DIM = 64
N_SSM_HEADS = 8
SSM_GROUPS = 2
SSM_STATE = 64
SSM_CONV = 5
SSM_CHUNK = 128
N_DIRS = 2
CONV_CH = 768
N_EXPERTS = 64
N_EXPERT_GROUPS = 8
TOPK_GROUPS = 4
TOP_K = 8
EXPERT_FF = 256
ROUTED_SCALE = 2.5
EPS = 1e-6
IN_W = 2320
IN_W_PAD = 2432

LANES = 128
SUBLANES = 8
TM = 256
TP = 512
HALO = 8
MOE_TS = 4096
MOE_BLK = 256
MOE_RMW_GROUP = 8
MOE_LIST_STEPS = 8
VMEM_LIMIT = 48 * 1024 * 1024

HIGHEST = lax.Precision.HIGHEST


def _cparams(sem):
    return pltpu.CompilerParams(dimension_semantics=sem, vmem_limit_bytes=VMEM_LIMIT)


def _silu(x):
    return x * jax.nn.sigmoid(x)


def _softplus(x):
    return jnp.maximum(x, 0.0) + jnp.log1p(jnp.exp(-jnp.abs(x)))


def _mod_kernel(c_ref, w_ref, b_ref, o_ref):
    c = c_ref[...]
    o_ref[0] = jnp.dot(_silu(c), w_ref[0], preferred_element_type=F32, precision=HIGHEST) + b_ref[0]


def _modulation(cond, w_mod, b_mod):
    L = w_mod.shape[0]
    bn = 1536
    return pl.pallas_call(
        _mod_kernel,
        grid=(L, 6 * D_MODEL // bn),
        in_specs=[
            pl.BlockSpec((SUBLANES, D_MODEL), lambda l, j: (0, 0)),
            pl.BlockSpec((1, D_MODEL, bn), lambda l, j: (l, 0, j)),
            pl.BlockSpec((1, 1, bn), lambda l, j: (l, 0, j)),
        ],
        out_specs=pl.BlockSpec((1, SUBLANES, bn), lambda l, j: (l, 0, j)),
        out_shape=jax.ShapeDtypeStruct((L, SUBLANES, 6 * D_MODEL), F32),
        compiler_params=_cparams(("parallel", "parallel")),
        name="adaln_mod",
    )(cond, w_mod, b_mod.reshape(L, 1, 6 * D_MODEL))


def _inproj_kernel(x_ref, mod_ref, g_ref, w_ref, wdt_ref, qg_ref, kg_ref, cos_ref, sin_ref, ones_ref,
                   pool_ref, qn_ref, kT_ref, kn_ref, v_ref, z_ref, xbc_ref, dt_ref, dtT_ref):
    x = x_ref[...]
    xn = x * lax.rsqrt(jnp.mean(x * x, axis=-1, keepdims=True) + EPS) * g_ref[...]
    h = xn * (1.0 + mod_ref[0, 1:2, :]) + mod_ref[0, 0:1, :]
    hb = h.astype(BF16)
    u = jnp.dot(hb, w_ref[...], preferred_element_type=F32)
    pool_ref[...] = u[:, 0:256]
    v_ref[...] = u[:, 768:1024]
    z_ref[...] = u[:, 1024:1536]
    xbc_ref[...] = u[:, 1536:2304]
    dt_ref[...] = u[:, 2304:2304 + LANES][:, 0:16]
    dtT_ref[...] = lax.dot_general(wdt_ref[...], hb, (((1,), (1,)), ((), ())), preferred_element_type=F32)

    lane = lax.broadcasted_iota(jnp.int32, (TP, QK_W), 1)
    first_half = (lane % ATT_QK_DIM) < (ATT_QK_DIM // 2)
    cos = cos_ref[...]
    sin = sin_ref[...]

    def qk_norm(t, gain):
        ss = jnp.dot(t * t, ones_ref[...], preferred_element_type=F32, precision=HIGHEST)
        return t * lax.rsqrt(ss * (1.0 / ATT_QK_DIM) + EPS) * gain

    def rope(t):
        partner = jnp.where(first_half, pltpu.roll(t, QK_W - ATT_QK_DIM // 2, 1), pltpu.roll(t, ATT_QK_DIM // 2, 1))
        return t * cos + partner * sin

    qn_ref[...] = rope(qk_norm(u[:, 256:512], qg_ref[...])).astype(BF16)
    kn = qk_norm(u[:, 512:768], kg_ref[...])
    kn_ref[...] = kn
    kT_ref[...] = rope(kn).T.astype(BF16)


def _in_projection(x, mod_l, norm_g, w_in_p, w_dtT, qg, kg, cos_t, sin_t, ones_bd, n_ctx_tiles, lat_tiles):
    n = x.shape[0]
    nt = n // TP

    def mod_row(i):
        return jnp.where(i < n_ctx_tiles, 0, 1 + (i - n_ctx_tiles) // lat_tiles)

    def rope_row(i):
        return jnp.where(i < n_ctx_tiles, lat_tiles, (i - n_ctx_tiles) % lat_tiles)

    row = lambda w: pl.BlockSpec((TP, w), lambda i: (i, 0))
    const = lambda a: pl.BlockSpec(a.shape, lambda i: (0,) * a.ndim)
    outs = [
        jax.ShapeDtypeStruct((n, POOL_W), F32),
        jax.ShapeDtypeStruct((n, QK_W), BF16),
        jax.ShapeDtypeStruct((QK_W, n), BF16),
        jax.ShapeDtypeStruct((n, QK_W), F32),
        jax.ShapeDtypeStruct((n, ATT_W), F32),
        jax.ShapeDtypeStruct((n, SSM_W), F32),
        jax.ShapeDtypeStruct((n, CONV_CH), F32),
        jax.ShapeDtypeStruct((n, 16), F32),
        jax.ShapeDtypeStruct((16, n), F32),
    ]
    return pl.pallas_call(
        _inproj_kernel,
        grid=(nt,),
        in_specs=[
            row(D_MODEL),
            pl.BlockSpec((1, 6, D_MODEL), lambda i: (mod_row(i), 0, 0)),
            const(norm_g), const(w_in_p), const(w_dtT), const(qg), const(kg),
            pl.BlockSpec((TP, QK_W), lambda i: (rope_row(i), 0)),
            pl.BlockSpec((TP, QK_W), lambda i: (rope_row(i), 0)),
            const(ones_bd),
        ],
        out_specs=[
            row(POOL_W), row(QK_W), pl.BlockSpec((QK_W, TP), lambda i: (0, i)), row(QK_W), row(ATT_W),
            row(SSM_W), row(CONV_CH), row(16), pl.BlockSpec((16, TP), lambda i: (0, i)),
        ],
        out_shape=outs,
        compiler_params=_cparams(("parallel",)),
        name="in_projection",
    )(x, mod_l, norm_g, w_in_p, w_dtT, qg, kg, cos_t, sin_t, ones_bd)


def _shift_kernel(seq_tiles_ref, pc_ref, pp_ref, pn_ref, xc_ref, xp_ref, xn_ref, wbd_ref, psc_ref, cw_ref, cb_ref,
                  pool_ref, xbc_ref):
    i = pl.program_id(0)
    st = seq_tiles_ref[i]
    pos = seq_tiles_ref[i + pl.num_programs(0)]
    has_prev = pos > 0
    has_next = pos < st - 1
    rows = TM + 2 * HALO
    r = lax.broadcasted_iota(jnp.int32, (rows, 1), 0)
    t = r - HALO + pos * TM
    t_len = st * TM

    def extended(c_ref, p_ref, n_ref):
        return jnp.concatenate([jnp.where(has_prev, p_ref[...], 0.0), c_ref[...],
                                jnp.where(has_next, n_ref[...], 0.0)], axis=0)

    def down(a, s):
        return pltpu.roll(a, s, 0)

    def up(a, s):
        return pltpu.roll(a, rows - s, 0)

    u = extended(pc_ref, pp_ref, pn_ref)
    w2 = u + down(u, 1)
    w4 = down(w2, 1) + up(w2, 1)
    w8 = down(w4, 2) + up(w4, 2)
    w16 = down(w8, 4) + up(w8, 4)
    lane = lax.broadcasted_iota(jnp.int32, (rows, POOL_W), 1)
    grp = lane // POOL_GROUP
    wsum = jnp.where(grp == 0, w2, jnp.where(grp == 1, w4, jnp.where(grp == 2, w8, w16)))
    half = jnp.where(grp == 0, 1, jnp.where(grp == 1, 2, jnp.where(grp == 2, 4, 8)))
    lo = jnp.maximum(t - half, 0)
    hi = jnp.minimum(t + half - 1, t_len - 1)
    cnt = (hi - lo + 1).astype(F32)
    pooled = (wsum / cnt - u)[HALO:HALO + TM]
    mixed = jnp.dot(pooled.astype(BF16), wbd_ref[...], preferred_element_type=F32)
    pool_ref[...] = mixed * psc_ref[...]

    x = extended(xc_ref, xp_ref, xn_ref)
    acc = x * cw_ref[2:3, :] + cb_ref[...]
    acc = acc + down(x, 2) * cw_ref[0:1, :] + down(x, 1) * cw_ref[1:2, :]
    acc = acc + up(x, 1) * cw_ref[3:4, :] + up(x, 2) * cw_ref[4:5, :]
    xbc_ref[...] = _silu(acc[HALO:HALO + TM])


def _token_shift(seq_info, pool_in, xbc, wbd, pool_scale, conv_w, conv_b):
    n = pool_in.shape[0]
    nt = n // TM
    hb = TM // HALO
    nhb = n // HALO

    cur = lambda w: pl.BlockSpec((TM, w), lambda i, s: (i, 0))
    prev = lambda w: pl.BlockSpec((HALO, w), lambda i, s: (jnp.maximum(i * hb - 1, 0), 0))
    nxt = lambda w: pl.BlockSpec((HALO, w), lambda i, s: (jnp.minimum((i + 1) * hb, nhb - 1), 0))
    const = lambda a: pl.BlockSpec(a.shape, lambda i, s: (0,) * a.ndim)
    grid_spec = pltpu.PrefetchScalarGridSpec(
        num_scalar_prefetch=1,
        grid=(nt,),
        in_specs=[cur(POOL_W), prev(POOL_W), nxt(POOL_W), cur(CONV_CH), prev(CONV_CH), nxt(CONV_CH),
                  const(wbd), const(pool_scale), const(conv_w), const(conv_b)],
        out_specs=[cur(POOL_W), cur(CONV_CH)],
    )
    return pl.pallas_call(
        _shift_kernel,
        grid_spec=grid_spec,
        out_shape=[jax.ShapeDtypeStruct((n, POOL_W), F32), jax.ShapeDtypeStruct((n, CONV_CH), F32)],
        compiler_params=_cparams(("parallel",)),
        name="token_shift",
    )(seq_info, pool_in, pool_in, pool_in, xbc, xbc, xbc, wbd, pool_scale, conv_w, conv_b)


def _attn_kernel(lam_ref, q_ref, kT_ref, v_ref, g_ref, o_ref, *, lam_init):
    lam = lam_ref[0]
    c0 = (ATT_QK_DIM ** -0.5) * math.log2(math.e)
    q = q_ref[...]
    outs = []
    for h in range(N_ATT_HEADS):
        parts = []
        for c in range(2):
            off = h * 2 * ATT_QK_DIM + c * ATT_QK_DIM
            s = jnp.dot(q[:, off:off + ATT_QK_DIM], kT_ref[0, off:off + ATT_QK_DIM, :], preferred_element_type=F32)
            m = jnp.max(s, axis=-1, keepdims=True)
            p = jnp.exp2((s - m) * c0)
            pv = jnp.dot(p.astype(BF16), v_ref[0, h], preferred_element_type=F32)
            parts.append(pv * (1.0 / jnp.sum(p, axis=-1, keepdims=True)))
        o = parts[0] - lam * parts[1]
        o = o * lax.rsqrt(jnp.mean(o * o, axis=-1, keepdims=True) + EPS) * g_ref[...] * (1.0 - lam_init)
        outs.append(o)
    o_ref[...] = jnp.concatenate(outs, axis=-1)


def _attention(lam, qn, kT, vh, subln_g, row0, n_seq, t_seq, lam_init):
    s_all = kT.shape[-1]
    tq = TM
    qb = t_seq // tq
    b0 = row0 // tq
    return pl.pallas_call(
        functools.partial(_attn_kernel, lam_init=lam_init),
        grid=(n_seq, qb),
        in_specs=[
            pl.BlockSpec(memory_space=pltpu.SMEM),
            pl.BlockSpec((tq, QK_W), lambda b, i: (b0 + b * qb + i, 0)),
            pl.BlockSpec((1, QK_W, s_all), lambda b, i: (b, 0, 0)),
            pl.BlockSpec((1, N_ATT_HEADS, s_all, ATT_V_DIM), lambda b, i: (b, 0, 0, 0)),
            pl.BlockSpec((1, ATT_V_DIM), lambda b, i: (0, 0)),
        ],
        out_specs=pl.BlockSpec((tq, ATT_W), lambda b, i: (b * qb + i, 0)),
        out_shape=jax.ShapeDtypeStruct((n_seq * t_seq, ATT_W), F32),
        compiler_params=_cparams(("parallel", "parallel")),
        name="diff_attention",
    )(lam, qn, kT, vh, subln_g)


def _ssd_chunk(d, xbc, dt_raw, dtT_raw, dtb_ref, dtbT_ref, ac_ref, acT_ref, h_scr):
    Q = SSM_CHUNK
    H = N_SSM_HEADS
    lo, hi = d * H, (d + 1) * H
    xs = xbc[:, :SSM_W]
    dt = _softplus(dt_raw[:, lo:hi] + dtb_ref[:, lo:hi])
    dtT = _softplus(dtT_raw[lo:hi] + dtbT_ref[lo:hi])
    a = dt * ac_ref[:, lo:hi]
    aT = dtT * acT_ref[lo:hi]
    li = lax.broadcasted_iota(jnp.int32, (Q, Q), 0)
    si = lax.broadcasted_iota(jnp.int32, (Q, Q), 1)
    valid = (li >= si) if d == 0 else (li <= si)
    tri = valid.astype(F32)
    cs = jnp.dot(tri, a, preferred_element_type=F32, precision=HIGHEST)
    csT = lax.dot_general(aT, tri, (((1,), (1,)), ((), ())), preferred_element_type=F32, precision=HIGHEST)
    totT = jnp.dot(aT, jnp.ones((Q, Q), F32), preferred_element_type=F32, precision=HIGHEST)
    eye = (lax.broadcasted_iota(jnp.int32, (SSM_STATE, SSM_STATE), 0)
           == lax.broadcasted_iota(jnp.int32, (SSM_STATE, SSM_STATE), 1)).astype(F32)

    ys = []
    per_group = N_SSM_HEADS // SSM_GROUPS
    for g in range(SSM_GROUPS):
        bg = xbc[:, SSM_W + g * SSM_STATE:SSM_W + (g + 1) * SSM_STATE]
        cg = xbc[:, SSM_W + SSM_GROUPS * SSM_STATE + g * SSM_STATE:SSM_W + SSM_GROUPS * SSM_STATE + (g + 1) * SSM_STATE]
        cgb = cg.astype(BF16)
        gram = lax.dot_general(cgb, bg.astype(BF16), (((1,), (1,)), ((), ())), preferred_element_type=F32)
        bgT = lax.dot_general(eye, bg, (((1,), (1,)), ((), ())), preferred_element_type=F32, precision=HIGHEST)
        for hh in range(per_group):
            h = g * per_group + hh
            col = jnp.broadcast_to(cs[:, h:h + 1], (Q, Q))
            rowv = csT[h:h + 1]
            dtr = dtT[h:h + 1]
            totr = totT[h:h + 1]
            decay = jnp.exp(jnp.where(valid, col - rowv, -jnp.inf))
            xh = xs[:, h * SSM_HEAD_DIM:(h + 1) * SSM_HEAD_DIM].astype(BF16)
            y = jnp.dot((gram * decay * dtr).astype(BF16), xh, preferred_element_type=F32)
            hT = h_scr[d, h]
            y = y + jnp.dot(cgb, hT.astype(BF16), preferred_element_type=F32) * jnp.exp(col[:, :SSM_HEAD_DIM])
            bdT = (bgT * (jnp.exp(totr - rowv) * dtr)).astype(BF16)
            upd = jnp.dot(bdT, xh, preferred_element_type=F32)
            h_scr[d, h] = jnp.exp(totr[:, :SSM_HEAD_DIM]) * hT + upd
            ys.append(y)
    return jnp.concatenate(ys, axis=-1)


def _ssd_kernel(sched_ref, xf_ref, xb_ref, dtf_ref, dtb_ref, dtTf_ref, dtTb_ref, bias_ref, biasT_ref, ac_ref, acT_ref,
                dsk_ref, h0_ref, yf_ref, yb_ref, hfin_ref, h_scr):
    s = pl.program_id(0)
    ns = pl.num_programs(0)
    first = sched_ref[2 * ns + s] == 1
    last = sched_ref[3 * ns + s] == 1

    @pl.when(first)
    def _():
        h_scr[...] = h0_ref[0]

    xf = xf_ref[...]
    yf = _ssd_chunk(0, xf, dtf_ref[...], dtTf_ref[...], bias_ref, biasT_ref, ac_ref, acT_ref, h_scr)
    yf_ref[...] = yf + xf[:, :SSM_W] * dsk_ref[...]
    yb_ref[...] = _ssd_chunk(1, xb_ref[...], dtb_ref[...], dtTb_ref[...], bias_ref, biasT_ref, ac_ref, acT_ref, h_scr)

    @pl.when(last)
    def _():
        hfin_ref[0] = h_scr[...]


def _ssd(sched, xbc_act, dt, dtT, dt_bias, dt_biasT, a_coef, a_coefT, d_skip_l, h0T, n_seq):
    n = xbc_act.shape[0]
    ns = sched.shape[0] // 5
    Q = SSM_CHUNK
    const = lambda a: pl.BlockSpec(a.shape, lambda s, sc: (0,) * a.ndim)
    fwd = lambda s, sc: sc[s]
    bwd = lambda s, sc: sc[ns + s]
    rows = lambda w, which: pl.BlockSpec((Q, w), lambda s, sc: (which(s, sc), 0))
    cols = lambda which: pl.BlockSpec((2 * N_SSM_HEADS, Q), lambda s, sc: (0, which(s, sc)))
    state_spec = pl.BlockSpec((1, N_DIRS, N_SSM_HEADS, SSM_STATE, SSM_HEAD_DIM),
                              lambda s, sc: (sc[4 * ns + s], 0, 0, 0, 0))
    grid_spec = pltpu.PrefetchScalarGridSpec(
        num_scalar_prefetch=1,
        grid=(ns,),
        in_specs=[
            rows(CONV_CH, fwd), rows(CONV_CH, bwd), rows(2 * N_SSM_HEADS, fwd), rows(2 * N_SSM_HEADS, bwd),
            cols(fwd), cols(bwd),
            const(dt_bias), const(dt_biasT), const(a_coef), const(a_coefT), const(d_skip_l),
            state_spec,
        ],
        out_specs=[rows(SSM_W, fwd), rows(SSM_W, bwd), state_spec],
        scratch_shapes=[pltpu.VMEM((N_DIRS, N_SSM_HEADS, SSM_STATE, SSM_HEAD_DIM), F32)],
    )
    return pl.pallas_call(
        _ssd_kernel,
        grid_spec=grid_spec,
        out_shape=[
            jax.ShapeDtypeStruct((n, SSM_W), F32),
            jax.ShapeDtypeStruct((n, SSM_W), F32),
            jax.ShapeDtypeStruct((n_seq, N_DIRS, N_SSM_HEADS, SSM_STATE, SSM_HEAD_DIM), F32),
        ],
        compiler_params=_cparams(("arbitrary",)),
        name="ssd_scan",
    )(sched, xbc_act, xbc_act, dt, dt, dtT, dtT, dt_bias, dt_biasT, a_coef, a_coefT, d_skip_l, h0T)


def _route_slots(scores, bias):
    tm = scores.shape[1]
    per_group = N_EXPERTS // N_EXPERT_GROUPS
    sel = scores + bias
    io_g = lax.broadcasted_iota(jnp.int32, (per_group, tm), 0)
    gscore = []
    for g in range(N_EXPERT_GROUPS):
        xg = sel[g * per_group:(g + 1) * per_group]
        m1 = jnp.max(xg, axis=0, keepdims=True)
        first = jnp.min(jnp.where(xg == m1, io_g, per_group), axis=0, keepdims=True)
        m2 = jnp.max(jnp.where(io_g == first, -jnp.inf, xg), axis=0, keepdims=True)
        gscore.append(m1 + m2)
    masked = []
    for g in range(N_EXPERT_GROUPS):
        beat = jnp.zeros((1, tm), jnp.int32)
        for o in range(N_EXPERT_GROUPS):
            if o == g:
                continue
            wins = (gscore[o] >= gscore[g]) if o < g else (gscore[o] > gscore[g])
            beat = beat + wins.astype(jnp.int32)
        masked.append(jnp.where(beat < TOPK_GROUPS, sel[g * per_group:(g + 1) * per_group], -jnp.inf))
    selm = jnp.concatenate(masked, axis=0)
    io_e = lax.broadcasted_iota(jnp.int32, (N_EXPERTS, tm), 0)
    rank = jnp.zeros((N_EXPERTS, tm), jnp.int32)
    for o in range(N_EXPERTS):
        row = selm[o:o + 1]
        rank = rank + jnp.where(row > selm, 1, jnp.where(row == selm, (io_e > o).astype(jnp.int32), 0))
    chosen = rank < TOP_K
    denom = jnp.sum(jnp.where(chosen, scores, 0.0), axis=0, keepdims=True)
    wdense = scores / denom * ROUTED_SCALE
    io_k = lax.broadcasted_iota(jnp.int32, (TOP_K, tm), 0)
    ids = jnp.zeros((TOP_K, tm), F32)
    wts = jnp.zeros((TOP_K, tm), F32)
    io_ef = io_e.astype(F32)
    for k in range(TOP_K):
        hit = rank == k
        ids = jnp.where(io_k == k, jnp.sum(jnp.where(hit, io_ef, 0.0), axis=0, keepdims=True), ids)
        wts = jnp.where(io_k == k, jnp.sum(jnp.where(hit, wdense, 0.0), axis=0, keepdims=True), wts)
    return ids.astype(jnp.int32), wts


def _outproj_kernel(x_ref, pool_ref, attc_ref, attl_ref, yf_ref, yb_ref, z_ref, sg_ref, w_ref, mod_ref, g2_ref, wr_ref, rb_ref,
                    x1_ref, h2_ref, eid_ref, ew_ref, *, n_ctx_tiles):
    att = jnp.where(pl.program_id(0) < n_ctx_tiles, attc_ref[...], attl_ref[...])
    y = (yf_ref[...] + yb_ref[...]) * _silu(z_ref[...])
    half = SSM_W // SSM_GROUPS
    parts = []
    for g in range(SSM_GROUPS):
        yg = y[:, g * half:(g + 1) * half]
        parts.append(yg * lax.rsqrt(jnp.mean(yg * yg, axis=-1, keepdims=True) + EPS))
    ssm = jnp.concatenate(parts, axis=-1) * sg_ref[...]
    cat = jnp.concatenate([pool_ref[...], att, ssm], axis=-1).astype(BF16)
    mixed = jnp.dot(cat, w_ref[...], preferred_element_type=F32)
    x1 = x_ref[...] + mod_ref[0, 2:3, :] * mixed
    x1_ref[...] = x1
    h2 = x1 * lax.rsqrt(jnp.mean(x1 * x1, axis=-1, keepdims=True) + EPS) * g2_ref[...]
    h2 = h2 * (1.0 + mod_ref[0, 4:5, :]) + mod_ref[0, 3:4, :]
    for j in range(D_MODEL // LANES):
        h2_ref[pl.ds(j, TP, stride=SUBLANES), :] = h2[:, j * LANES:(j + 1) * LANES]
    logits = lax.dot_general(wr_ref[...], h2, (((1,), (1,)), ((), ())), preferred_element_type=F32, precision=HIGHEST)
    ids, wts = _route_slots(jax.nn.sigmoid(logits), rb_ref[...])
    eid_ref[...] = ids
    ew_ref[...] = wts


def _out_projection(x, pool_out, att_ctx, att_lat, yf, yb, z, ssm_g, w_out, mod_l, norm2_g, w_rT, r_bias, n_ctx_tiles,
                    lat_tiles):
    n = x.shape[0]
    nt = n // TP

    def mod_row(i):
        return jnp.where(i < n_ctx_tiles, 0, 1 + (i - n_ctx_tiles) // lat_tiles)

    row = lambda w: pl.BlockSpec((TP, w), lambda i: (i, 0))
    const = lambda a: pl.BlockSpec(a.shape, lambda i: (0,) * a.ndim)
    return pl.pallas_call(
        functools.partial(_outproj_kernel, n_ctx_tiles=n_ctx_tiles),
        grid=(nt,),
        in_specs=[
            row(D_MODEL), row(POOL_W),
            pl.BlockSpec((TP, ATT_W), lambda i: (jnp.minimum(i, n_ctx_tiles - 1), 0)),
            pl.BlockSpec((TP, ATT_W), lambda i: (jnp.maximum(i - n_ctx_tiles, 0), 0)),
            row(SSM_W), row(SSM_W), row(SSM_W), const(ssm_g), const(w_out),
            pl.BlockSpec((1, 6, D_MODEL), lambda i: (mod_row(i), 0, 0)),
            const(norm2_g), const(w_rT), const(r_bias),
        ],
        out_specs=[row(D_MODEL), pl.BlockSpec((TP * SUBLANES, LANES), lambda i: (i, 0)),
                   pl.BlockSpec((TOP_K, TP), lambda i: (0, i)), pl.BlockSpec((TOP_K, TP), lambda i: (0, i))],
        out_shape=[
            jax.ShapeDtypeStruct((n, D_MODEL), F32),
            jax.ShapeDtypeStruct((n * SUBLANES, LANES), F32),
            jax.ShapeDtypeStruct((TOP_K, n), jnp.int32),
            jax.ShapeDtypeStruct((TOP_K, n), F32),
        ],
        compiler_params=_cparams(("parallel",)),
        name="out_projection",
    )(x, pool_out, att_ctx, att_lat, yf, yb, z, ssm_g, w_out, mod_l, norm2_g, w_rT, r_bias)


def _moe_kernel(meta_ref, src_ref, dst_ref, pdst_ref, wrow_ref, x_ref, wgu_ref, wd_ref, acc_ref, xg, ys_even, ys_odd):
    b = pl.program_id(0)
    nb = pl.num_programs(0)
    valid = meta_ref[4 * nb + b] == 1
    first = meta_ref[5 * nb + b] == 1
    flush = meta_ref[6 * nb + b] == 1
    even = b % 2 == 0
    R = MOE_BLK
    sub = meta_ref[2 * nb + b] % MOE_LIST_STEPS
    base = sub * R
    pbase = (meta_ref[3 * nb + b] % MOE_LIST_STEPS) * R
    S = R + 1
    nch = D_MODEL // LANES

    @pl.when(b == 0)
    def _():
        ys_even[...] = jnp.zeros_like(ys_even)
        ys_odd[...] = jnp.zeros_like(ys_odd)

    @pl.when(first)
    def _():
        acc_ref[...] = jnp.zeros_like(acc_ref)

    def scatter_add(ys, rows_ref, off, after=None):
        for g0 in range(0, R, MOE_RMW_GROUP):
            pend = []
            for r in range(g0, g0 + MOE_RMW_GROUP):
                dst = pl.multiple_of(rows_ref[0, 0, off + r], SUBLANES)
                row = ys[pl.ds(r, nch, stride=S), :]
                if after is not None:
                    row = row + after
                pend.append((dst, acc_ref[0, pl.ds(dst, SUBLANES), :] + row))
            for dst, val in pend:
                acc_ref[0, pl.ds(dst, SUBLANES), :] = val

    def step(ys_prev, ys_cur):
        for r in range(R):
            src = pl.multiple_of(src_ref[0, 0, base + r], SUBLANES)
            xg[pl.ds(r, nch, stride=S), :] = x_ref[0, pl.ds(src, SUBLANES), :]
        xb = jnp.concatenate([xg[j * S:j * S + R, :] for j in range(nch)], axis=-1).astype(BF16)
        hgu = jnp.dot(xb, wgu_ref[0], preferred_element_type=F32)
        bits = pltpu.bitcast(hgu[0:SUBLANES, 0:LANES], jnp.uint32)
        zero = pltpu.bitcast(lax.shift_right_logical(bits, jnp.uint32(32)), F32)
        scatter_add(ys_prev, pdst_ref, pbase, after=zero)
        hm = (_silu(hgu[:, :EXPERT_FF]) * hgu[:, EXPERT_FF:]).astype(BF16)
        y = jnp.dot(hm, wd_ref[0], preferred_element_type=F32)
        ii = lax.broadcasted_iota(jnp.int32, (R, R), 0)
        jj = lax.broadcasted_iota(jnp.int32, (R, R), 1)
        w_row = jnp.broadcast_to(wrow_ref[0, pl.ds(sub, 1), :], (R, R))
        w_col = jnp.sum(jnp.where(ii == jj, w_row, 0.0), axis=1, keepdims=True)
        y = y * w_col
        for j in range(nch):
            ys_cur[j * S:j * S + R, :] = y[:, j * LANES:(j + 1) * LANES]

    @pl.when(jnp.logical_and(valid, even))
    def _():
        step(ys_odd, ys_even)

    @pl.when(jnp.logical_and(valid, jnp.logical_not(even)))
    def _():
        step(ys_even, ys_odd)

    @pl.when(jnp.logical_and(flush, even))
    def _():
        scatter_add(ys_even, dst_ref, base)

    @pl.when(jnp.logical_and(flush, jnp.logical_not(even)))
    def _():
        scatter_add(ys_odd, dst_ref, base)


def _routed_experts(meta, src_rows, dst_rows, wrow, x_rows, w_gateup, w_down, n_tiles):
    nb = meta.shape[0] // 7
    R = MOE_BLK
    L = MOE_LIST_STEPS
    rows_in = MOE_TS * SUBLANES
    rows_out = (MOE_TS + SUBLANES) * SUBLANES
    grid_spec = pltpu.PrefetchScalarGridSpec(
        num_scalar_prefetch=1,
        grid=(nb,),
        in_specs=[
            pl.BlockSpec((1, 1, L * R), lambda b, m: (m[2 * nb + b] // L, 0, 0), memory_space=pltpu.SMEM),
            pl.BlockSpec((1, 1, L * R), lambda b, m: (m[2 * nb + b] // L, 0, 0), memory_space=pltpu.SMEM),
            pl.BlockSpec((1, 1, L * R), lambda b, m: (m[3 * nb + b] // L, 0, 0), memory_space=pltpu.SMEM),
            pl.BlockSpec((1, L, R), lambda b, m: (m[2 * nb + b] // L, 0, 0)),
            pl.BlockSpec((1, rows_in, LANES), lambda b, m: (m[b], 0, 0), pipeline_mode=pl.Buffered(1)),
            pl.BlockSpec((1, D_MODEL, 2 * EXPERT_FF), lambda b, m: (m[nb + b], 0, 0)),
            pl.BlockSpec((1, EXPERT_FF, D_MODEL), lambda b, m: (m[nb + b], 0, 0)),
        ],
        out_specs=pl.BlockSpec((1, rows_out, LANES), lambda b, m: (m[b], 0, 0), pipeline_mode=pl.Buffered(1)),
        scratch_shapes=[pltpu.VMEM(((R + 1) * SUBLANES, LANES), F32) for _ in range(3)],
    )
    return pl.pallas_call(
        _moe_kernel,
        grid_spec=grid_spec,
        out_shape=jax.ShapeDtypeStruct((n_tiles, rows_out, LANES), F32),
        compiler_params=_cparams(("arbitrary",)),
        name="routed_experts",
    )(meta, src_rows, dst_rows, dst_rows, wrow, x_rows, w_gateup, w_down)


def _dispatch_plan(eids, ewts):
    n = eids.shape[1]
    s = n * TOP_K
    n_tiles = n // MOE_TS
    nkeys = n_tiles * N_EXPERTS
    tok_bits = (MOE_TS + SUBLANES - 1).bit_length()
    tok = jnp.arange(n, dtype=jnp.int32)
    key = (tok // MOE_TS)[None, :] * N_EXPERTS + eids
    experts = jnp.arange(N_EXPERTS, dtype=jnp.int32)[:, None, None]
    counts = jnp.sum((eids[None] == experts).reshape(N_EXPERTS, TOP_K, n_tiles, MOE_TS), axis=(1, 3), dtype=jnp.int32)
    counts = counts.T.reshape(nkeys)
    padded = (counts + MOE_BLK - 1) // MOE_BLK * MOE_BLK
    j = jnp.arange(MOE_BLK - 1, dtype=jnp.int32)[None, :]
    keys = jnp.arange(nkeys, dtype=jnp.int32)[:, None]
    pad_code = (keys << (tok_bits + 1)) | (1 << tok_bits) | (MOE_TS + j % SUBLANES)
    pad_code = jnp.where(j < (padded - counts)[:, None], pad_code, jnp.iinfo(jnp.int32).max)
    code = (key << (tok_bits + 1)) | (tok % MOE_TS)[None, :]
    n_slots = s + nkeys * (MOE_BLK - 1)
    nb = -(-n_slots // (MOE_BLK * MOE_LIST_STEPS)) * MOE_LIST_STEPS
    tail = nb * MOE_BLK - n_slots
    all_code = jnp.concatenate([code.reshape(-1), pad_code.reshape(-1),
                                jnp.full((tail,), jnp.iinfo(jnp.int32).max, jnp.int32)])
    all_w = jnp.concatenate([ewts.reshape(-1), jnp.zeros((nkeys * (MOE_BLK - 1) + tail,), F32)])
    scode, sw = lax.sort((all_code, all_w), num_keys=1)
    stok = scode & ((1 << tok_bits) - 1)
    pend = jnp.cumsum(padded)
    used = pend[-1] // MOE_BLK
    step = jnp.arange(nb, dtype=jnp.int32)
    valid = step < used
    blk = jnp.minimum(step, used - 1)
    k = jnp.searchsorted(pend, blk * MOE_BLK, side="right").astype(jnp.int32)
    btile = k // N_EXPERTS
    bexp = k % N_EXPERTS
    changes = btile[1:] != btile[:-1]
    first = valid & jnp.concatenate([jnp.ones((1,), bool), changes])
    flush = valid & jnp.concatenate([changes | ~valid[1:], jnp.ones((1,), bool)])
    pending = jnp.concatenate([jnp.zeros((1,), bool), valid[:-1] & ~flush[:-1]]) & valid
    pblk = jnp.where(pending, jnp.concatenate([blk[:1], blk[:-1]]), nb)
    meta = jnp.concatenate([btile, bexp, blk, pblk, valid.astype(jnp.int32), first.astype(jnp.int32),
                            flush.astype(jnp.int32)])
    src_rows = jnp.where(stok < MOE_TS, stok, 0) * SUBLANES
    chunk = MOE_BLK * MOE_LIST_STEPS
    spare = (MOE_TS + jnp.arange(chunk, dtype=jnp.int32) % SUBLANES) * SUBLANES
    dst_rows = jnp.concatenate([stok * SUBLANES, spare])
    nchunks = nb // MOE_LIST_STEPS
    return (meta.astype(jnp.int32), src_rows.reshape(nchunks, 1, chunk), dst_rows.reshape(nchunks + 1, 1, chunk),
            sw.reshape(nchunks, MOE_LIST_STEPS, MOE_BLK))


def _ffn_out_kernel(x1_ref, h2_ref, r_ref, wg_ref, wu_ref, wd_ref, mod_ref, o_ref):
    def token_major(ref2d):
        return jnp.concatenate([ref2d[pl.ds(j, TP, stride=SUBLANES), :] for j in range(D_MODEL // LANES)], axis=-1)

    hb = token_major(h2_ref).astype(BF16)
    routed = token_major(r_ref.at[0])
    hg = jnp.dot(hb, wg_ref[...], preferred_element_type=F32)
    hu = jnp.dot(hb, wu_ref[...], preferred_element_type=F32)
    shared = jnp.dot((_silu(hg) * hu).astype(BF16), wd_ref[...], preferred_element_type=F32)
    o_ref[...] = x1_ref[...] + mod_ref[0, 5:6, :] * (routed + shared)


def _ffn_out(x1, h2, routed, ws_gate, ws_up, ws_down, mod_l, n_ctx_tiles, lat_tiles):
    n = x1.shape[0]
    nt = n // TP
    tiles_per_super = MOE_TS // TP

    def mod_row(i):
        return jnp.where(i < n_ctx_tiles, 0, 1 + (i - n_ctx_tiles) // lat_tiles)

    row = lambda w: pl.BlockSpec((TP, w), lambda i: (i, 0))
    const = lambda a: pl.BlockSpec(a.shape, lambda i: (0,) * a.ndim)
    return pl.pallas_call(
        _ffn_out_kernel,
        grid=(nt,),
        in_specs=[row(D_MODEL), pl.BlockSpec((TP * SUBLANES, LANES), lambda i: (i, 0)),
                  pl.BlockSpec((1, TP * SUBLANES, LANES), lambda i: (i // tiles_per_super, i % tiles_per_super, 0)),
                  const(ws_gate), const(ws_up), const(ws_down),
                  pl.BlockSpec((1, 6, D_MODEL), lambda i: (mod_row(i), 0, 0))],
        out_specs=row(D_MODEL),
        out_shape=jax.ShapeDtypeStruct((n, D_MODEL), F32),
        compiler_params=_cparams(("parallel",)),
        name="ffn_out",
    )(x1, h2, routed, ws_gate, ws_up, ws_down, mod_l)


def _rope_tables(t_seq):
    rows = t_seq // GRID_W
    r, col = jnp.meshgrid(jnp.arange(rows), jnp.arange(GRID_W), indexing="ij")
    pos = jnp.stack([r.reshape(-1), col.reshape(-1)], axis=-1).astype(F32)
    inv = ROPE_THETA ** (-jnp.arange(ROPE_FREQS, dtype=F32) / ROPE_FREQS)
    ang = pos[:, :, None] * inv
    cos = jnp.cos(ang).reshape(t_seq, 2 * ROPE_FREQS)
    sin = jnp.sin(ang).reshape(t_seq, 2 * ROPE_FREQS)
    cos32 = jnp.concatenate([cos, cos], axis=-1)
    sin32 = jnp.concatenate([-sin, sin], axis=-1)
    reps = QK_W // ATT_QK_DIM
    cos_t = jnp.tile(cos32, (1, reps))
    sin_t = jnp.tile(sin32, (1, reps))
    cos_t = jnp.concatenate([cos_t, jnp.ones((TP, QK_W), F32)], axis=0)
    sin_t = jnp.concatenate([sin_t, jnp.zeros((TP, QK_W), F32)], axis=0)
    return cos_t, sin_t


def _block_diag(blocks):
    g, a, b = blocks.shape
    out = jnp.zeros((g * a, g * b), blocks.dtype)
    for i in range(g):
        out = out.at[i * a:(i + 1) * a, i * b:(i + 1) * b].set(blocks[i])
    return out


def _ssd_schedule(n_ctx_seq, ctx_chunks, n_lat_seq, lat_chunks):
    fblk, bblk, first, last, seq = [], [], [], [], []
    base = 0
    sid = 0
    for n_seq, nc in ((n_ctx_seq, ctx_chunks), (n_lat_seq, lat_chunks)):
        for _ in range(n_seq):
            for k in range(nc):
                fblk.append(base + k)
                bblk.append(base + nc - 1 - k)
                first.append(int(k == 0))
                last.append(int(k == nc - 1))
                seq.append(sid)
            base += nc
            sid += 1
    return jnp.asarray(fblk + bblk + first + last + seq, jnp.int32)


def kernel(x_prompt, x_sample, cache_k, cache_v, state_ssm, c, c_ctx, w_mod, b_mod, norm1_g, w_in, pool_w, pool_scale, q_norm_g, k_norm_g, lambda_q, lambda_k, subln_g, conv_w, conv_b, dt_bias, a_log, d_skip, ssm_norm_g, w_out, norm2_g, w_router, router_bias, w_gate, w_up, w_down, ws_gate, ws_up, ws_down):
    nb_ctx, t_ctx, d = x_prompt.shape
    nb_lat, t_lat, _ = x_sample.shape
    depth = w_mod.shape[0]
    past = cache_k.shape[2]
    n_ctx = nb_ctx * t_ctx
    n_lat = nb_lat * t_lat
    n = n_ctx + n_lat
    assert d == D_MODEL and t_ctx % TM == 0 and t_lat % TM == 0 and n % MOE_TS == 0
    assert n_ctx % TP == 0 and t_lat % TP == 0 and MOE_TS % TP == 0
    assert 1 + nb_lat <= SUBLANES
    n_ctx_tiles = n_ctx // TM
    lat_tiles = t_lat // TM
    p_ctx_tiles = n_ctx // TP
    p_lat_tiles = t_lat // TP
    n_seq = nb_ctx + nb_lat

    x = jnp.concatenate([x_prompt.reshape(n_ctx, d), x_sample.reshape(n_lat, d)], axis=0)
    cond = jnp.zeros((SUBLANES, d), F32).at[0].set(c_ctx).at[1:1 + nb_lat].set(c)
    mod = _modulation(cond, w_mod, b_mod).reshape(depth, SUBLANES, 6, d)

    cos_t, sin_t = _rope_tables(t_lat)
    ones_bd = _block_diag(jnp.ones((QK_W // ATT_QK_DIM, ATT_QK_DIM, ATT_QK_DIM), F32))
    seq_tiles = jnp.asarray([t_ctx // TM] * n_ctx_tiles + [lat_tiles] * (nb_lat * lat_tiles), jnp.int32)
    tile_pos = jnp.asarray([i % (t_ctx // TM) for i in range(n_ctx_tiles)] + [i % lat_tiles for i in range(nb_lat * lat_tiles)], jnp.int32)
    seq_info = jnp.concatenate([seq_tiles, tile_pos])
    sched = _ssd_schedule(nb_ctx, t_ctx // SSM_CHUNK, nb_lat, t_lat // SSM_CHUNK)

    new_k, new_v, new_ssm = [], [], []
    for l in range(depth):
        mod_l = mod[l]
        w_in_p = jnp.pad(w_in[l], ((0, 0), (0, IN_W_PAD - IN_W))).astype(BF16)
        w_dtT = w_in[l][:, IN_W - 16:].T.astype(BF16)
        qg = jnp.tile(q_norm_g[l], QK_W // ATT_QK_DIM)[None]
        kg = jnp.tile(k_norm_g[l], QK_W // ATT_QK_DIM)[None]
        pool_in, qn, kT, kn, v, z, xbc, dt, dtT = _in_projection(
            x, mod_l, norm1_g[l][None], w_in_p, w_dtT, qg, kg, cos_t, sin_t, ones_bd, p_ctx_tiles, p_lat_tiles)

        wbd = _block_diag(pool_w[l]).astype(BF16)
        pool_out, xbc_act = _token_shift(seq_info, pool_in, xbc, wbd, pool_scale[l][None], conv_w[l], conv_b[l][None])

        lam_init = 0.8 - 0.6 * math.exp(-0.3 * l)
        lq = lambda_q[l].astype(F32)
        lk = lambda_k[l].astype(F32)
        lam = (jnp.exp(jnp.sum(lq[0] * lk[0])) - jnp.exp(jnp.sum(lq[1] * lk[1])) + lam_init).reshape(1)
        vb = v.astype(BF16)
        kT_ctx = kT[:, :n_ctx].reshape(QK_W, nb_ctx, t_ctx).transpose(1, 0, 2)
        v_ctx = vb[:n_ctx].reshape(nb_ctx, t_ctx, N_ATT_HEADS, ATT_V_DIM).transpose(0, 2, 1, 3)
        kT_lat = kT[:, n_ctx:].reshape(QK_W, nb_lat, t_lat).transpose(1, 0, 2)
        ck = cache_k[:, l].reshape(nb_lat, past, QK_W).transpose(0, 2, 1).astype(BF16)
        kT_lat = jnp.concatenate([kT_lat, ck], axis=2)
        v_lat = vb[n_ctx:].reshape(nb_lat, t_lat, N_ATT_HEADS, ATT_V_DIM).transpose(0, 2, 1, 3)
        cv = cache_v[:, l].astype(BF16).transpose(0, 2, 1, 3)
        v_lat = jnp.concatenate([v_lat, cv], axis=2)
        sg = subln_g[l][None]
        att_ctx = _attention(lam, qn, kT_ctx, v_ctx, sg, 0, nb_ctx, t_ctx, lam_init)
        att_lat = _attention(lam, qn, kT_lat, v_lat, sg, n_ctx, nb_lat, t_lat, lam_init)

        h0T = jnp.concatenate([
            jnp.zeros((nb_ctx, N_DIRS, N_SSM_HEADS, SSM_STATE, SSM_HEAD_DIM), F32),
            state_ssm[:, l].astype(F32).transpose(0, 1, 2, 4, 3)], axis=0)
        dtb = dt_bias[l].reshape(1, N_DIRS * N_SSM_HEADS)
        ac = (-jnp.exp(a_log[l].astype(F32))).reshape(1, N_DIRS * N_SSM_HEADS)
        dsk = jnp.repeat(d_skip[l].astype(F32), SSM_HEAD_DIM)[None]
        yf, yb, hfinT = _ssd(sched, xbc_act, dt, dtT, dtb, dtb.T, ac, ac.T, dsk, h0T, n_seq)

        x1, h2, eids, ewts = _out_projection(
            x, pool_out, att_ctx, att_lat, yf, yb, z, ssm_norm_g[l][None], w_out[l].astype(BF16), mod_l, norm2_g[l][None],
            w_router[l].T, router_bias[l].astype(F32)[:, None], p_ctx_tiles, p_lat_tiles)

        meta, src_rows, dst_rows, w_list = _dispatch_plan(eids, ewts)
        n_tiles = n // MOE_TS
        x_rows = h2.reshape(n_tiles, MOE_TS * SUBLANES, LANES)
        w_gateup = jnp.concatenate([w_gate[l].astype(BF16), w_up[l].astype(BF16)], axis=-1)
        acc = _routed_experts(meta, src_rows, dst_rows, w_list, x_rows, w_gateup, w_down[l].astype(BF16), n_tiles)
        x = _ffn_out(x1, h2, acc, ws_gate[l].astype(BF16), ws_up[l].astype(BF16), ws_down[l].astype(BF16),
                     mod_l, p_ctx_tiles, p_lat_tiles)

        new_k.append(kn[:n_ctx].reshape(nb_ctx, t_ctx, N_ATT_HEADS, 2, ATT_QK_DIM))
        new_v.append(v[:n_ctx].reshape(nb_ctx, t_ctx, N_ATT_HEADS, ATT_V_DIM))
        new_ssm.append(hfinT[:nb_ctx].transpose(0, 1, 2, 4, 3))

    y_prompt = x[:n_ctx].reshape(nb_ctx, t_ctx, d)
    y_sample = x[n_ctx:].reshape(nb_lat, t_lat, d)
    return (y_prompt, y_sample, jnp.stack(new_k, axis=1), jnp.stack(new_v, axis=1), jnp.stack(new_ssm, axis=1))
```

```python
import functools
import math

import jax
import jax.numpy as jnp
from jax import lax
from jax.experimental import pallas as pl
from jax.experimental.pallas import tpu as pltpu

F32 = jnp.float32
BF16 = jnp.bfloat16

D_MODEL = 1024
GRID_W = 64
POOL_W = 256
POOL_WINDOWS = (2, 4, 8, 16)
POOL_GROUP = 64
N_ATT_HEADS = 4
ATT_V_DIM = 64
ATT_QK_DIM = 32
ATT_W = 256
QK_W = 256
ROPE_FREQS = 8
ROPE_THETA = 10000.0
SSM_W = 512
SSM_HEAD_DIM = 64
N_SSM_HEADS = 8
SSM_GROUPS = 2
SSM_STATE = 64
SSM_CONV = 5
SSM_CHUNK = 128
N_DIRS = 2
CONV_CH = 768
N_EXPERTS = 64
N_EXPERT_GROUPS = 8
TOPK_GROUPS = 4
TOP_K = 8
EXPERT_FF = 256
ROUTED_SCALE = 2.5
EPS = 1e-6
IN_W = 2320
IN_W_PAD = 2432

LANES = 128
SUBLANES = 8
TM = 256
TP = 512
HALO = 8
MOE_TS = 4096
MOE_BLK = 256
MOE_RMW_GROUP = 8
MOE_LIST_STEPS = 1
VMEM_LIMIT = 48 * 1024 * 1024

HIGHEST = lax.Precision.HIGHEST


def _cparams(sem):
    return pltpu.CompilerParams(dimension_semantics=sem, vmem_limit_bytes=VMEM_LIMIT)


def _silu(x):
    return x * jax.nn.sigmoid(x)


def _softplus(x):
    return jnp.maximum(x, 0.0) + jnp.log1p(jnp.exp(-jnp.abs(x)))


def _mod_kernel(c_ref, w_ref, b_ref, o_ref):
    c = c_ref[...]
    o_ref[0] = jnp.dot(_silu(c), w_ref[0], preferred_element_type=F32, precision=HIGHEST) + b_ref[0]


def _modulation(cond, w_mod, b_mod):
    L = w_mod.shape[0]
    bn = 1536
    return pl.pallas_call(
        _mod_kernel,
        grid=(L, 6 * D_MODEL // bn),
        in_specs=[
            pl.BlockSpec((SUBLANES, D_MODEL), lambda l, j: (0, 0)),
            pl.BlockSpec((1, D_MODEL, bn), lambda l, j: (l, 0, j)),
            pl.BlockSpec((1, 1, bn), lambda l, j: (l, 0, j)),
        ],
        out_specs=pl.BlockSpec((1, SUBLANES, bn), lambda l, j: (l, 0, j)),
        out_shape=jax.ShapeDtypeStruct((L, SUBLANES, 6 * D_MODEL), F32),
        compiler_params=_cparams(("parallel", "parallel")),
        name="adaln_mod",
    )(cond, w_mod, b_mod.reshape(L, 1, 6 * D_MODEL))


def _inproj_kernel(x_ref, mod_ref, g_ref, w_ref, wdt_ref, qg_ref, kg_ref, cos_ref, sin_ref, ones_ref,
                   pool_ref, qn_ref, kT_ref, kn_ref, v_ref, z_ref, xbc_ref, dt_ref, dtT_ref):
    x = x_ref[...]
    xn = x * lax.rsqrt(jnp.mean(x * x, axis=-1, keepdims=True) + EPS) * g_ref[...]
    h = xn * (1.0 + mod_ref[0, 1:2, :]) + mod_ref[0, 0:1, :]
    hb = h.astype(BF16)
    u = jnp.dot(hb, w_ref[...], preferred_element_type=F32)
    pool_ref[...] = u[:, 0:256]
    v_ref[...] = u[:, 768:1024]
    z_ref[...] = u[:, 1024:1536]
    xbc_ref[...] = u[:, 1536:2304]
    dt_ref[...] = u[:, 2304:2304 + LANES][:, 0:16]
    dtT_ref[...] = lax.dot_general(wdt_ref[...], hb, (((1,), (1,)), ((), ())), preferred_element_type=F32)

    lane = lax.broadcasted_iota(jnp.int32, (TP, QK_W), 1)
    first_half = (lane % ATT_QK_DIM) < (ATT_QK_DIM // 2)
    cos = cos_ref[...]
    sin = sin_ref[...]

    def qk_norm(t, gain):
        sq = t * t
        hi = sq.astype(BF16)
        lo = (sq - hi.astype(F32)).astype(BF16)
        ones = ones_ref[...]
        ss = jnp.dot(hi, ones, preferred_element_type=F32) + jnp.dot(lo, ones, preferred_element_type=F32)
        return t * lax.rsqrt(ss * (1.0 / ATT_QK_DIM) + EPS) * gain

    def rope(t):
        partner = jnp.where(first_half, pltpu.roll(t, QK_W - ATT_QK_DIM // 2, 1), pltpu.roll(t, ATT_QK_DIM // 2, 1))
        return t * cos + partner * sin

    qn_ref[...] = rope(qk_norm(u[:, 256:512], qg_ref[...])).astype(BF16)
    kn = qk_norm(u[:, 512:768], kg_ref[...])
    kn_ref[...] = kn
    kT_ref[...] = rope(kn).T.astype(BF16)


def _in_projection(x, mod_l, norm_g, w_in_p, w_dtT, qg, kg, cos_t, sin_t, ones_bd, n_ctx_tiles, lat_tiles):
    n = x.shape[0]
    nt = n // TP

    def mod_row(i):
        return jnp.where(i < n_ctx_tiles, 0, 1 + (i - n_ctx_tiles) // lat_tiles)

    def rope_row(i):
        return jnp.where(i < n_ctx_tiles, lat_tiles, (i - n_ctx_tiles) % lat_tiles)

    row = lambda w: pl.BlockSpec((TP, w), lambda i: (i, 0))
    const = lambda a: pl.BlockSpec(a.shape, lambda i: (0,) * a.ndim)
    outs = [
        jax.ShapeDtypeStruct((n, POOL_W), F32),
        jax.ShapeDtypeStruct((n, QK_W), BF16),
        jax.ShapeDtypeStruct((QK_W, n), BF16),
        jax.ShapeDtypeStruct((n, QK_W), F32),
        jax.ShapeDtypeStruct((n, ATT_W), F32),
        jax.ShapeDtypeStruct((n, SSM_W), F32),
        jax.ShapeDtypeStruct((n, CONV_CH), F32),
        jax.ShapeDtypeStruct((n, 16), F32),
        jax.ShapeDtypeStruct((16, n), F32),
    ]
    return pl.pallas_call(
        _inproj_kernel,
        grid=(nt,),
        in_specs=[
            row(D_MODEL),
            pl.BlockSpec((1, 6, D_MODEL), lambda i: (mod_row(i), 0, 0)),
            const(norm_g), const(w_in_p), const(w_dtT), const(qg), const(kg),
            pl.BlockSpec((TP, QK_W), lambda i: (rope_row(i), 0)),
            pl.BlockSpec((TP, QK_W), lambda i: (rope_row(i), 0)),
            const(ones_bd),
        ],
        out_specs=[
            row(POOL_W), row(QK_W), pl.BlockSpec((QK_W, TP), lambda i: (0, i)), row(QK_W), row(ATT_W),
            row(SSM_W), row(CONV_CH), row(16), pl.BlockSpec((16, TP), lambda i: (0, i)),
        ],
        out_shape=outs,
        compiler_params=_cparams(("parallel",)),
        name="in_projection",
    )(x, mod_l, norm_g, w_in_p, w_dtT, qg, kg, cos_t, sin_t, ones_bd)


def _shift_kernel(seq_tiles_ref, pc_ref, pp_ref, pn_ref, xc_ref, xp_ref, xn_ref, wbd_ref, psc_ref, cw_ref, cb_ref,
                  pool_ref, xbc_ref):
    i = pl.program_id(0)
    st = seq_tiles_ref[i]
    pos = seq_tiles_ref[i + pl.num_programs(0)]
    has_prev = pos > 0
    has_next = pos < st - 1
    rows = TM + 2 * HALO
    r = lax.broadcasted_iota(jnp.int32, (rows, 1), 0)
    t = r - HALO + pos * TM
    t_len = st * TM

    def extended(c_ref, p_ref, n_ref):
        return jnp.concatenate([jnp.where(has_prev, p_ref[...], 0.0), c_ref[...],
                                jnp.where(has_next, n_ref[...], 0.0)], axis=0)

    def down(a, s):
        return pltpu.roll(a, s, 0)

    def up(a, s):
        return pltpu.roll(a, rows - s, 0)

    u = extended(pc_ref, pp_ref, pn_ref)
    w2 = u + down(u, 1)
    w4 = down(w2, 1) + up(w2, 1)
    w8 = down(w4, 2) + up(w4, 2)
    w16 = down(w8, 4) + up(w8, 4)
    lane = lax.broadcasted_iota(jnp.int32, (rows, POOL_W), 1)
    grp = lane // POOL_GROUP
    wsum = jnp.where(grp == 0, w2, jnp.where(grp == 1, w4, jnp.where(grp == 2, w8, w16)))
    half = jnp.where(grp == 0, 1, jnp.where(grp == 1, 2, jnp.where(grp == 2, 4, 8)))
    lo = jnp.maximum(t - half, 0)
    hi = jnp.minimum(t + half - 1, t_len - 1)
    cnt = (hi - lo + 1).astype(F32)
    pooled = (wsum / cnt - u)[HALO:HALO + TM]
    mixed = jnp.dot(pooled.astype(BF16), wbd_ref[...], preferred_element_type=F32)
    pool_ref[...] = mixed * psc_ref[...]

    x = extended(xc_ref, xp_ref, xn_ref)
    acc = x * cw_ref[2:3, :] + cb_ref[...]
    acc = acc + down(x, 2) * cw_ref[0:1, :] + down(x, 1) * cw_ref[1:2, :]
    acc = acc + up(x, 1) * cw_ref[3:4, :] + up(x, 2) * cw_ref[4:5, :]
    xbc_ref[...] = _silu(acc[HALO:HALO + TM])


def _token_shift(seq_info, pool_in, xbc, wbd, pool_scale, conv_w, conv_b):
    n = pool_in.shape[0]
    nt = n // TM
    hb = TM // HALO
    nhb = n // HALO

    cur = lambda w: pl.BlockSpec((TM, w), lambda i, s: (i, 0))
    prev = lambda w: pl.BlockSpec((HALO, w), lambda i, s: (jnp.maximum(i * hb - 1, 0), 0))
    nxt = lambda w: pl.BlockSpec((HALO, w), lambda i, s: (jnp.minimum((i + 1) * hb, nhb - 1), 0))
    const = lambda a: pl.BlockSpec(a.shape, lambda i, s: (0,) * a.ndim)
    grid_spec = pltpu.PrefetchScalarGridSpec(
        num_scalar_prefetch=1,
        grid=(nt,),
        in_specs=[cur(POOL_W), prev(POOL_W), nxt(POOL_W), cur(CONV_CH), prev(CONV_CH), nxt(CONV_CH),
                  const(wbd), const(pool_scale), const(conv_w), const(conv_b)],
        out_specs=[cur(POOL_W), cur(CONV_CH)],
    )
    return pl.pallas_call(
        _shift_kernel,
        grid_spec=grid_spec,
        out_shape=[jax.ShapeDtypeStruct((n, POOL_W), F32), jax.ShapeDtypeStruct((n, CONV_CH), F32)],
        compiler_params=_cparams(("parallel",)),
        name="token_shift",
    )(seq_info, pool_in, pool_in, pool_in, xbc, xbc, xbc, wbd, pool_scale, conv_w, conv_b)


def _attn_kernel(lam_ref, q_ref, kT_ref, v_ref, g_ref, o_ref, *, lam_init):
    lam = lam_ref[0]
    c0 = (ATT_QK_DIM ** -0.5) * math.log2(math.e)
    q = q_ref[...]
    outs = []
    for h in range(N_ATT_HEADS):
        parts = []
        for c in range(2):
            off = h * 2 * ATT_QK_DIM + c * ATT_QK_DIM
            s = jnp.dot(q[:, off:off + ATT_QK_DIM], kT_ref[0, off:off + ATT_QK_DIM, :], preferred_element_type=F32)
            m = jnp.max(s, axis=-1, keepdims=True)
            p = jnp.exp2((s - m) * c0)
            pv = jnp.dot(p.astype(BF16), v_ref[0, h], preferred_element_type=F32)
            parts.append(pv * (1.0 / jnp.sum(p, axis=-1, keepdims=True)))
        o = parts[0] - lam * parts[1]
        o = o * lax.rsqrt(jnp.mean(o * o, axis=-1, keepdims=True) + EPS) * g_ref[...] * (1.0 - lam_init)
        outs.append(o)
    o_ref[...] = jnp.concatenate(outs, axis=-1)


def _attention(lam, qn, kT, vh, subln_g, row0, n_seq, t_seq, lam_init):
    s_all = kT.shape[-1]
    tq = TM
    qb = t_seq // tq
    b0 = row0 // tq
    return pl.pallas_call(
        functools.partial(_attn_kernel, lam_init=lam_init),
        grid=(n_seq, qb),
        in_specs=[
            pl.BlockSpec(memory_space=pltpu.SMEM),
            pl.BlockSpec((tq, QK_W), lambda b, i: (b0 + b * qb + i, 0)),
            pl.BlockSpec((1, QK_W, s_all), lambda b, i: (b, 0, 0)),
            pl.BlockSpec((1, N_ATT_HEADS, s_all, ATT_V_DIM), lambda b, i: (b, 0, 0, 0)),
            pl.BlockSpec((1, ATT_V_DIM), lambda b, i: (0, 0)),
        ],
        out_specs=pl.BlockSpec((tq, ATT_W), lambda b, i: (b * qb + i, 0)),
        out_shape=jax.ShapeDtypeStruct((n_seq * t_seq, ATT_W), F32),
        compiler_params=_cparams(("parallel", "parallel")),
        name="diff_attention",
    )(lam, qn, kT, vh, subln_g)


def _ssd_chunk(d, xbc, dt_raw, dtT_raw, dtb_ref, dtbT_ref, ac_ref, acT_ref, h_scr):
    Q = SSM_CHUNK
    H = N_SSM_HEADS
    lo, hi = d * H, (d + 1) * H
    xs = xbc[:, :SSM_W]
    dt = _softplus(dt_raw[:, lo:hi] + dtb_ref[:, lo:hi])
    dtT = _softplus(dtT_raw[lo:hi] + dtbT_ref[lo:hi])
    a = dt * ac_ref[:, lo:hi]
    aT = dtT * acT_ref[lo:hi]
    li = lax.broadcasted_iota(jnp.int32, (Q, Q), 0)
    si = lax.broadcasted_iota(jnp.int32, (Q, Q), 1)
    valid = (li >= si) if d == 0 else (li <= si)
    tri = valid.astype(F32)
    cs = jnp.dot(tri, a, preferred_element_type=F32, precision=HIGHEST)
    csT = lax.dot_general(aT, tri, (((1,), (1,)), ((), ())), preferred_element_type=F32, precision=HIGHEST)
    totT = jnp.dot(aT, jnp.ones((Q, Q), F32), preferred_element_type=F32, precision=HIGHEST)
    eye = (lax.broadcasted_iota(jnp.int32, (SSM_STATE, SSM_STATE), 0)
           == lax.broadcasted_iota(jnp.int32, (SSM_STATE, SSM_STATE), 1)).astype(F32)

    ys = []
    per_group = N_SSM_HEADS // SSM_GROUPS
    for g in range(SSM_GROUPS):
        bg = xbc[:, SSM_W + g * SSM_STATE:SSM_W + (g + 1) * SSM_STATE]
        cg = xbc[:, SSM_W + SSM_GROUPS * SSM_STATE + g * SSM_STATE:SSM_W + SSM_GROUPS * SSM_STATE + (g + 1) * SSM_STATE]
        cgb = cg.astype(BF16)
        gram = lax.dot_general(cgb, bg.astype(BF16), (((1,), (1,)), ((), ())), preferred_element_type=F32)
        bgT = lax.dot_general(eye, bg, (((1,), (1,)), ((), ())), preferred_element_type=F32, precision=HIGHEST)
        for hh in range(per_group):
            h = g * per_group + hh
            col = jnp.broadcast_to(cs[:, h:h + 1], (Q, Q))
            rowv = csT[h:h + 1]
            dtr = dtT[h:h + 1]
            totr = totT[h:h + 1]
            decay = jnp.exp(jnp.where(valid, col - rowv, -jnp.inf))
            xh = xs[:, h * SSM_HEAD_DIM:(h + 1) * SSM_HEAD_DIM].astype(BF16)
            y = jnp.dot((gram * decay * dtr).astype(BF16), xh, preferred_element_type=F32)
            hT = h_scr[d, h]
            y = y + jnp.dot(cgb, hT.astype(BF16), preferred_element_type=F32) * jnp.exp(col[:, :SSM_HEAD_DIM])
            bdT = (bgT * (jnp.exp(totr - rowv) * dtr)).astype(BF16)
            upd = jnp.dot(bdT, xh, preferred_element_type=F32)
            h_scr[d, h] = jnp.exp(totr[:, :SSM_HEAD_DIM]) * hT + upd
            ys.append(y)
    return jnp.concatenate(ys, axis=-1)


def _ssd_kernel(sched_ref, xf_ref, xb_ref, dtf_ref, dtb_ref, dtTf_ref, dtTb_ref, bias_ref, biasT_ref, ac_ref, acT_ref,
                dsk_ref, h0_ref, yf_ref, yb_ref, hfin_ref, h_scr):
    s = pl.program_id(0)
    ns = pl.num_programs(0)
    first = sched_ref[2 * ns + s] == 1
    last = sched_ref[3 * ns + s] == 1

    @pl.when(first)
    def _():
        h_scr[...] = h0_ref[0]

    xf = xf_ref[...]
    yf = _ssd_chunk(0, xf, dtf_ref[...], dtTf_ref[...], bias_ref, biasT_ref, ac_ref, acT_ref, h_scr)
    yf_ref[...] = yf + xf[:, :SSM_W] * dsk_ref[...]
    yb_ref[...] = _ssd_chunk(1, xb_ref[...], dtb_ref[...], dtTb_ref[...], bias_ref, biasT_ref, ac_ref, acT_ref, h_scr)

    @pl.when(last)
    def _():
        hfin_ref[0] = h_scr[...]


def _ssd(sched, xbc_act, dt, dtT, dt_bias, dt_biasT, a_coef, a_coefT, d_skip_l, h0T, n_seq):
    n = xbc_act.shape[0]
    ns = sched.shape[0] // 5
    Q = SSM_CHUNK
    const = lambda a: pl.BlockSpec(a.shape, lambda s, sc: (0,) * a.ndim)
    fwd = lambda s, sc: sc[s]
    bwd = lambda s, sc: sc[ns + s]
    rows = lambda w, which: pl.BlockSpec((Q, w), lambda s, sc: (which(s, sc), 0))
    cols = lambda which: pl.BlockSpec((2 * N_SSM_HEADS, Q), lambda s, sc: (0, which(s, sc)))
    state_spec = pl.BlockSpec((1, N_DIRS, N_SSM_HEADS, SSM_STATE, SSM_HEAD_DIM),
                              lambda s, sc: (sc[4 * ns + s], 0, 0, 0, 0))
    grid_spec = pltpu.PrefetchScalarGridSpec(
        num_scalar_prefetch=1,
        grid=(ns,),
        in_specs=[
            rows(CONV_CH, fwd), rows(CONV_CH, bwd), rows(2 * N_SSM_HEADS, fwd), rows(2 * N_SSM_HEADS, bwd),
            cols(fwd), cols(bwd),
            const(dt_bias), const(dt_biasT), const(a_coef), const(a_coefT), const(d_skip_l),
            state_spec,
        ],
        out_specs=[rows(SSM_W, fwd), rows(SSM_W, bwd), state_spec],
        scratch_shapes=[pltpu.VMEM((N_DIRS, N_SSM_HEADS, SSM_STATE, SSM_HEAD_DIM), F32)],
    )
    return pl.pallas_call(
        _ssd_kernel,
        grid_spec=grid_spec,
        out_shape=[
            jax.ShapeDtypeStruct((n, SSM_W), F32),
            jax.ShapeDtypeStruct((n, SSM_W), F32),
            jax.ShapeDtypeStruct((n_seq, N_DIRS, N_SSM_HEADS, SSM_STATE, SSM_HEAD_DIM), F32),
        ],
        compiler_params=_cparams(("arbitrary",)),
        name="ssd_scan",
    )(sched, xbc_act, xbc_act, dt, dt, dtT, dtT, dt_bias, dt_biasT, a_coef, a_coefT, d_skip_l, h0T)


def _route_slots(scores, bias):
    tm = scores.shape[1]
    per_group = N_EXPERTS // N_EXPERT_GROUPS
    sel = scores + bias
    io_g = lax.broadcasted_iota(jnp.int32, (per_group, tm), 0)
    gscore = []
    for g in range(N_EXPERT_GROUPS):
        xg = sel[g * per_group:(g + 1) * per_group]
        m1 = jnp.max(xg, axis=0, keepdims=True)
        first = jnp.min(jnp.where(xg == m1, io_g, per_group), axis=0, keepdims=True)
        m2 = jnp.max(jnp.where(io_g == first, -jnp.inf, xg), axis=0, keepdims=True)
        gscore.append(m1 + m2)
    masked = []
    for g in range(N_EXPERT_GROUPS):
        beat = jnp.zeros((1, tm), jnp.int32)
        for o in range(N_EXPERT_GROUPS):
            if o == g:
                continue
            wins = (gscore[o] >= gscore[g]) if o < g else (gscore[o] > gscore[g])
            beat = beat + wins.astype(jnp.int32)
        masked.append(jnp.where(beat < TOPK_GROUPS, sel[g * per_group:(g + 1) * per_group], -jnp.inf))
    selm = jnp.concatenate(masked, axis=0)
    io_e = lax.broadcasted_iota(jnp.int32, (N_EXPERTS, tm), 0)
    rank = jnp.zeros((N_EXPERTS, tm), jnp.int32)
    for o in range(N_EXPERTS):
        row = selm[o:o + 1]
        rank = rank + jnp.where(row > selm, 1, jnp.where(row == selm, (io_e > o).astype(jnp.int32), 0))
    chosen = rank < TOP_K
    denom = jnp.sum(jnp.where(chosen, scores, 0.0), axis=0, keepdims=True)
    wdense = scores / denom * ROUTED_SCALE
    io_k = lax.broadcasted_iota(jnp.int32, (TOP_K, tm), 0)
    ids = jnp.zeros((TOP_K, tm), F32)
    wts = jnp.zeros((TOP_K, tm), F32)
    io_ef = io_e.astype(F32)
    for k in range(TOP_K):
        hit = rank == k
        ids = jnp.where(io_k == k, jnp.sum(jnp.where(hit, io_ef, 0.0), axis=0, keepdims=True), ids)
        wts = jnp.where(io_k == k, jnp.sum(jnp.where(hit, wdense, 0.0), axis=0, keepdims=True), wts)
    return ids.astype(jnp.int32), wts


def _outproj_kernel(x_ref, pool_ref, attc_ref, attl_ref, yf_ref, yb_ref, z_ref, sg_ref, w_ref, mod_ref, g2_ref, wr_ref, rb_ref,
                    x1_ref, h2_ref, eid_ref, ew_ref, *, n_ctx_tiles):
    att = jnp.where(pl.program_id(0) < n_ctx_tiles, attc_ref[...], attl_ref[...])
    y = (yf_ref[...] + yb_ref[...]) * _silu(z_ref[...])
    half = SSM_W // SSM_GROUPS
    parts = []
    for g in range(SSM_GROUPS):
        yg = y[:, g * half:(g + 1) * half]
        parts.append(yg * lax.rsqrt(jnp.mean(yg * yg, axis=-1, keepdims=True) + EPS))
    ssm = jnp.concatenate(parts, axis=-1) * sg_ref[...]
    cat = jnp.concatenate([pool_ref[...], att, ssm], axis=-1).astype(BF16)
    mixed = jnp.dot(cat, w_ref[...], preferred_element_type=F32)
    x1 = x_ref[...] + mod_ref[0, 2:3, :] * mixed
    x1_ref[...] = x1
    h2 = x1 * lax.rsqrt(jnp.mean(x1 * x1, axis=-1, keepdims=True) + EPS) * g2_ref[...]
    h2 = h2 * (1.0 + mod_ref[0, 4:5, :]) + mod_ref[0, 3:4, :]
    for j in range(D_MODEL // LANES):
        h2_ref[pl.ds(j, TP, stride=SUBLANES), :] = h2[:, j * LANES:(j + 1) * LANES]
    logits = lax.dot_general(wr_ref[...], h2, (((1,), (1,)), ((), ())), preferred_element_type=F32, precision=HIGHEST)
    ids, wts = _route_slots(jax.nn.sigmoid(logits), rb_ref[...])
    eid_ref[...] = ids
    ew_ref[...] = wts


def _out_projection(x, pool_out, att_ctx, att_lat, yf, yb, z, ssm_g, w_out, mod_l, norm2_g, w_rT, r_bias, n_ctx_tiles,
                    lat_tiles):
    n = x.shape[0]
    nt = n // TP

    def mod_row(i):
        return jnp.where(i < n_ctx_tiles, 0, 1 + (i - n_ctx_tiles) // lat_tiles)

    row = lambda w: pl.BlockSpec((TP, w), lambda i: (i, 0))
    const = lambda a: pl.BlockSpec(a.shape, lambda i: (0,) * a.ndim)
    return pl.pallas_call(
        functools.partial(_outproj_kernel, n_ctx_tiles=n_ctx_tiles),
        grid=(nt,),
        in_specs=[
            row(D_MODEL), row(POOL_W),
            pl.BlockSpec((TP, ATT_W), lambda i: (jnp.minimum(i, n_ctx_tiles - 1), 0)),
            pl.BlockSpec((TP, ATT_W), lambda i: (jnp.maximum(i - n_ctx_tiles, 0), 0)),
            row(SSM_W), row(SSM_W), row(SSM_W), const(ssm_g), const(w_out),
            pl.BlockSpec((1, 6, D_MODEL), lambda i: (mod_row(i), 0, 0)),
            const(norm2_g), const(w_rT), const(r_bias),
        ],
        out_specs=[row(D_MODEL), pl.BlockSpec((TP * SUBLANES, LANES), lambda i: (i, 0)),
                   pl.BlockSpec((TOP_K, TP), lambda i: (0, i)), pl.BlockSpec((TOP_K, TP), lambda i: (0, i))],
        out_shape=[
            jax.ShapeDtypeStruct((n, D_MODEL), F32),
            jax.ShapeDtypeStruct((n * SUBLANES, LANES), F32),
            jax.ShapeDtypeStruct((TOP_K, n), jnp.int32),
            jax.ShapeDtypeStruct((TOP_K, n), F32),
        ],
        compiler_params=_cparams(("parallel",)),
        name="out_projection",
    )(x, pool_out, att_ctx, att_lat, yf, yb, z, ssm_g, w_out, mod_l, norm2_g, w_rT, r_bias)


def _moe_kernel(meta_ref, src_ref, dst_ref, pdst_ref, wrow_ref, x_ref, wgu_ref, wd_ref, acc_ref, xg, ys_even, ys_odd):
    b = pl.program_id(0)
    nb = pl.num_programs(0)
    valid = meta_ref[4 * nb + b] == 1
    first = meta_ref[5 * nb + b] == 1
    flush = meta_ref[6 * nb + b] == 1
    even = b % 2 == 0
    R = MOE_BLK
    if MOE_LIST_STEPS == 1:
        sub = base = pbase = 0
    else:
        sub = meta_ref[2 * nb + b] % MOE_LIST_STEPS
        base = sub * R
        pbase = (meta_ref[3 * nb + b] % MOE_LIST_STEPS) * R
    S = R + 1
    nch = D_MODEL // LANES

    @pl.when(b == 0)
    def _():
        ys_even[...] = jnp.zeros_like(ys_even)
        ys_odd[...] = jnp.zeros_like(ys_odd)

    @pl.when(first)
    def _():
        acc_ref[...] = jnp.zeros_like(acc_ref)

    def scatter_add(ys, rows_ref, off, after=None):
        for g0 in range(0, R, MOE_RMW_GROUP):
            pend = []
            for r in range(g0, g0 + MOE_RMW_GROUP):
                dst = pl.multiple_of(rows_ref[0, 0, off + r], SUBLANES)
                row = ys[pl.ds(r, nch, stride=S), :]
                if after is not None:
                    row = row + after
                pend.append((dst, acc_ref[0, pl.ds(dst, SUBLANES), :] + row))
            for dst, val in pend:
                acc_ref[0, pl.ds(dst, SUBLANES), :] = val

    def step(ys_prev, ys_cur):
        for r in range(R):
            src = pl.multiple_of(src_ref[0, 0, base + r], SUBLANES)
            xg[pl.ds(r, nch, stride=S), :] = x_ref[0, pl.ds(src, SUBLANES), :]
        xb = jnp.concatenate([xg[j * S:j * S + R, :] for j in range(nch)], axis=-1).astype(BF16)
        hgu = jnp.dot(xb, wgu_ref[0], preferred_element_type=F32)
        bits = pltpu.bitcast(hgu[0:SUBLANES, 0:LANES], jnp.uint32)
        zero = pltpu.bitcast(lax.shift_right_logical(bits, jnp.uint32(32)), F32)
        scatter_add(ys_prev, pdst_ref, pbase, after=zero)
        hm = (_silu(hgu[:, :EXPERT_FF]) * hgu[:, EXPERT_FF:]).astype(BF16)
        y = jnp.dot(hm, wd_ref[0], preferred_element_type=F32)
        ii = lax.broadcasted_iota(jnp.int32, (R, R), 0)
        jj = lax.broadcasted_iota(jnp.int32, (R, R), 1)
        w_row = jnp.broadcast_to(wrow_ref[0, pl.ds(sub, 1), :], (R, R))
        w_col = jnp.sum(jnp.where(ii == jj, w_row, 0.0), axis=1, keepdims=True)
        y = y * w_col
        for j in range(nch):
            ys_cur[j * S:j * S + R, :] = y[:, j * LANES:(j + 1) * LANES]

    @pl.when(jnp.logical_and(valid, even))
    def _():
        step(ys_odd, ys_even)

    @pl.when(jnp.logical_and(valid, jnp.logical_not(even)))
    def _():
        step(ys_even, ys_odd)

    @pl.when(jnp.logical_and(flush, even))
    def _():
        scatter_add(ys_even, dst_ref, base)

    @pl.when(jnp.logical_and(flush, jnp.logical_not(even)))
    def _():
        scatter_add(ys_odd, dst_ref, base)


def _routed_experts(meta, src_rows, dst_rows, wrow, x_rows, w_gateup, w_down, n_tiles):
    nb = meta.shape[0] // 7
    R = MOE_BLK
    L = MOE_LIST_STEPS
    rows_in = MOE_TS * SUBLANES
    rows_out = (MOE_TS + SUBLANES) * SUBLANES
    grid_spec = pltpu.PrefetchScalarGridSpec(
        num_scalar_prefetch=1,
        grid=(nb,),
        in_specs=[
            pl.BlockSpec((1, 1, L * R), lambda b, m: (m[2 * nb + b] // L, 0, 0), memory_space=pltpu.SMEM),
            pl.BlockSpec((1, 1, L * R), lambda b, m: (m[2 * nb + b] // L, 0, 0), memory_space=pltpu.SMEM),
            pl.BlockSpec((1, 1, L * R), lambda b, m: (m[3 * nb + b] // L, 0, 0), memory_space=pltpu.SMEM),
            pl.BlockSpec((1, L, R), lambda b, m: (m[2 * nb + b] // L, 0, 0)),
            pl.BlockSpec((1, rows_in, LANES), lambda b, m: (m[b], 0, 0), pipeline_mode=pl.Buffered(1)),
            pl.BlockSpec((1, D_MODEL, 2 * EXPERT_FF), lambda b, m: (m[nb + b], 0, 0)),
            pl.BlockSpec((1, EXPERT_FF, D_MODEL), lambda b, m: (m[nb + b], 0, 0)),
        ],
        out_specs=pl.BlockSpec((1, rows_out, LANES), lambda b, m: (m[b], 0, 0), pipeline_mode=pl.Buffered(1)),
        scratch_shapes=[pltpu.VMEM(((R + 1) * SUBLANES, LANES), F32) for _ in range(3)],
    )
    return pl.pallas_call(
        _moe_kernel,
        grid_spec=grid_spec,
        out_shape=jax.ShapeDtypeStruct((n_tiles, rows_out, LANES), F32),
        compiler_params=_cparams(("arbitrary",)),
        name="routed_experts",
    )(meta, src_rows, dst_rows, dst_rows, wrow, x_rows, w_gateup, w_down)


def _dispatch_plan(eids, ewts):
    n = eids.shape[1]
    s = n * TOP_K
    n_tiles = n // MOE_TS
    nkeys = n_tiles * N_EXPERTS
    tok_bits = (MOE_TS + SUBLANES - 1).bit_length()
    tok = jnp.arange(n, dtype=jnp.int32)
    key = (tok // MOE_TS)[None, :] * N_EXPERTS + eids
    experts = jnp.arange(N_EXPERTS, dtype=jnp.int32)[:, None, None]
    counts = jnp.sum((eids[None] == experts).reshape(N_EXPERTS, TOP_K, n_tiles, MOE_TS), axis=(1, 3), dtype=jnp.int32)
    counts = counts.T.reshape(nkeys)
    padded = (counts + MOE_BLK - 1) // MOE_BLK * MOE_BLK
    j = jnp.arange(MOE_BLK - 1, dtype=jnp.int32)[None, :]
    keys = jnp.arange(nkeys, dtype=jnp.int32)[:, None]
    pad_code = (keys << (tok_bits + 1)) | (1 << tok_bits) | (MOE_TS + j % SUBLANES)
    pad_code = jnp.where(j < (padded - counts)[:, None], pad_code, jnp.iinfo(jnp.int32).max)
    code = (key << (tok_bits + 1)) | (tok % MOE_TS)[None, :]
    n_slots = s + nkeys * (MOE_BLK - 1)
    nb = -(-n_slots // (MOE_BLK * MOE_LIST_STEPS)) * MOE_LIST_STEPS
    tail = nb * MOE_BLK - n_slots
    all_code = jnp.concatenate([code.reshape(-1), pad_code.reshape(-1),
                                jnp.full((tail,), jnp.iinfo(jnp.int32).max, jnp.int32)])
    all_w = jnp.concatenate([ewts.reshape(-1), jnp.zeros((nkeys * (MOE_BLK - 1) + tail,), F32)])
    scode, sw = lax.sort((all_code, all_w), num_keys=1)
    stok = scode & ((1 << tok_bits) - 1)
    pend = jnp.cumsum(padded)
    used = pend[-1] // MOE_BLK
    step = jnp.arange(nb, dtype=jnp.int32)
    valid = step < used
    blk = jnp.minimum(step, used - 1)
    k = jnp.searchsorted(pend, blk * MOE_BLK, side="right").astype(jnp.int32)
    btile = k // N_EXPERTS
    bexp = k % N_EXPERTS
    changes = btile[1:] != btile[:-1]
    first = valid & jnp.concatenate([jnp.ones((1,), bool), changes])
    flush = valid & jnp.concatenate([changes | ~valid[1:], jnp.ones((1,), bool)])
    pending = jnp.concatenate([jnp.zeros((1,), bool), valid[:-1] & ~flush[:-1]]) & valid
    pblk = jnp.where(pending, jnp.concatenate([blk[:1], blk[:-1]]), nb)
    meta = jnp.concatenate([btile, bexp, blk, pblk, valid.astype(jnp.int32), first.astype(jnp.int32),
                            flush.astype(jnp.int32)])
    src_rows = jnp.where(stok < MOE_TS, stok, 0) * SUBLANES
    chunk = MOE_BLK * MOE_LIST_STEPS
    spare = (MOE_TS + jnp.arange(chunk, dtype=jnp.int32) % SUBLANES) * SUBLANES
    dst_rows = jnp.concatenate([stok * SUBLANES, spare])
    nchunks = nb // MOE_LIST_STEPS
    return (meta.astype(jnp.int32), src_rows.reshape(nchunks, 1, chunk), dst_rows.reshape(nchunks + 1, 1, chunk),
            sw.reshape(nchunks, MOE_LIST_STEPS, MOE_BLK))


def _ffn_out_kernel(x1_ref, h2_ref, r_ref, wg_ref, wu_ref, wd_ref, mod_ref, o_ref):
    def token_major(ref2d):
        return jnp.concatenate([ref2d[pl.ds(j, TP, stride=SUBLANES), :] for j in range(D_MODEL // LANES)], axis=-1)

    hb = token_major(h2_ref).astype(BF16)
    routed = token_major(r_ref.at[0])
    hg = jnp.dot(hb, wg_ref[...], preferred_element_type=F32)
    hu = jnp.dot(hb, wu_ref[...], preferred_element_type=F32)
    shared = jnp.dot((_silu(hg) * hu).astype(BF16), wd_ref[...], preferred_element_type=F32)
    o_ref[...] = x1_ref[...] + mod_ref[0, 5:6, :] * (routed + shared)


def _ffn_out(x1, h2, routed, ws_gate, ws_up, ws_down, mod_l, n_ctx_tiles, lat_tiles):
    n = x1.shape[0]
    nt = n // TP
    tiles_per_super = MOE_TS // TP

    def mod_row(i):
        return jnp.where(i < n_ctx_tiles, 0, 1 + (i - n_ctx_tiles) // lat_tiles)

    row = lambda w: pl.BlockSpec((TP, w), lambda i: (i, 0))
    const = lambda a: pl.BlockSpec(a.shape, lambda i: (0,) * a.ndim)
    return pl.pallas_call(
        _ffn_out_kernel,
        grid=(nt,),
        in_specs=[row(D_MODEL), pl.BlockSpec((TP * SUBLANES, LANES), lambda i: (i, 0)),
                  pl.BlockSpec((1, TP * SUBLANES, LANES), lambda i: (i // tiles_per_super, i % tiles_per_super, 0)),
                  const(ws_gate), const(ws_up), const(ws_down),
                  pl.BlockSpec((1, 6, D_MODEL), lambda i: (mod_row(i), 0, 0))],
        out_specs=row(D_MODEL),
        out_shape=jax.ShapeDtypeStruct((n, D_MODEL), F32),
        compiler_params=_cparams(("parallel",)),
        name="ffn_out",
    )(x1, h2, routed, ws_gate, ws_up, ws_down, mod_l)


def _rope_tables(t_seq):
    rows = t_seq // GRID_W
    r, col = jnp.meshgrid(jnp.arange(rows), jnp.arange(GRID_W), indexing="ij")
    pos = jnp.stack([r.reshape(-1), col.reshape(-1)], axis=-1).astype(F32)
    inv = ROPE_THETA ** (-jnp.arange(ROPE_FREQS, dtype=F32) / ROPE_FREQS)
    ang = pos[:, :, None] * inv
    cos = jnp.cos(ang).reshape(t_seq, 2 * ROPE_FREQS)
    sin = jnp.sin(ang).reshape(t_seq, 2 * ROPE_FREQS)
    cos32 = jnp.concatenate([cos, cos], axis=-1)
    sin32 = jnp.concatenate([-sin, sin], axis=-1)
    reps = QK_W // ATT_QK_DIM
    cos_t = jnp.tile(cos32, (1, reps))
    sin_t = jnp.tile(sin32, (1, reps))
    cos_t = jnp.concatenate([cos_t, jnp.ones((TP, QK_W), F32)], axis=0)
    sin_t = jnp.concatenate([sin_t, jnp.zeros((TP, QK_W), F32)], axis=0)
    return cos_t, sin_t


def _block_diag(blocks):
    g, a, b = blocks.shape
    out = jnp.zeros((g * a, g * b), blocks.dtype)
    for i in range(g):
        out = out.at[i * a:(i + 1) * a, i * b:(i + 1) * b].set(blocks[i])
    return out


def _ssd_schedule(n_ctx_seq, ctx_chunks, n_lat_seq, lat_chunks):
    fblk, bblk, first, last, seq = [], [], [], [], []
    base = 0
    sid = 0
    for n_seq, nc in ((n_ctx_seq, ctx_chunks), (n_lat_seq, lat_chunks)):
        for _ in range(n_seq):
            for k in range(nc):
                fblk.append(base + k)
                bblk.append(base + nc - 1 - k)
                first.append(int(k == 0))
                last.append(int(k == nc - 1))
                seq.append(sid)
            base += nc
            sid += 1
    return jnp.asarray(fblk + bblk + first + last + seq, jnp.int32)


def kernel(x_prompt, x_sample, cache_k, cache_v, state_ssm, c, c_ctx, w_mod, b_mod, norm1_g, w_in, pool_w, pool_scale, q_norm_g, k_norm_g, lambda_q, lambda_k, subln_g, conv_w, conv_b, dt_bias, a_log, d_skip, ssm_norm_g, w_out, norm2_g, w_router, router_bias, w_gate, w_up, w_down, ws_gate, ws_up, ws_down):
    nb_ctx, t_ctx, d = x_prompt.shape
    nb_lat, t_lat, _ = x_sample.shape
    depth = w_mod.shape[0]
    past = cache_k.shape[2]
    n_ctx = nb_ctx * t_ctx
    n_lat = nb_lat * t_lat
    n = n_ctx + n_lat
    assert d == D_MODEL and t_ctx % TM == 0 and t_lat % TM == 0 and n % MOE_TS == 0
    assert n_ctx % TP == 0 and t_lat % TP == 0 and MOE_TS % TP == 0
    assert 1 + nb_lat <= SUBLANES
    n_ctx_tiles = n_ctx // TM
    lat_tiles = t_lat // TM
    p_ctx_tiles = n_ctx // TP
    p_lat_tiles = t_lat // TP
    n_seq = nb_ctx + nb_lat

    x = jnp.concatenate([x_prompt.reshape(n_ctx, d), x_sample.reshape(n_lat, d)], axis=0)
    cond = jnp.zeros((SUBLANES, d), F32).at[0].set(c_ctx).at[1:1 + nb_lat].set(c)
    mod = _modulation(cond, w_mod, b_mod).reshape(depth, SUBLANES, 6, d)

    cos_t, sin_t = _rope_tables(t_lat)
    ones_bd = _block_diag(jnp.ones((QK_W // ATT_QK_DIM, ATT_QK_DIM, ATT_QK_DIM), BF16))
    seq_tiles = jnp.asarray([t_ctx // TM] * n_ctx_tiles + [lat_tiles] * (nb_lat * lat_tiles), jnp.int32)
    tile_pos = jnp.asarray([i % (t_ctx // TM) for i in range(n_ctx_tiles)] + [i % lat_tiles for i in range(nb_lat * lat_tiles)], jnp.int32)
    seq_info = jnp.concatenate([seq_tiles, tile_pos])
    sched = _ssd_schedule(nb_ctx, t_ctx // SSM_CHUNK, nb_lat, t_lat // SSM_CHUNK)

    new_k, new_v, new_ssm = [], [], []
    for l in range(depth):
        mod_l = mod[l]
        w_in_p = jnp.pad(w_in[l], ((0, 0), (0, IN_W_PAD - IN_W))).astype(BF16)
        w_dtT = w_in[l][:, IN_W - 16:].T.astype(BF16)
        qg = jnp.tile(q_norm_g[l], QK_W // ATT_QK_DIM)[None]
        kg = jnp.tile(k_norm_g[l], QK_W // ATT_QK_DIM)[None]
        pool_in, qn, kT, kn, v, z, xbc, dt, dtT = _in_projection(
            x, mod_l, norm1_g[l][None], w_in_p, w_dtT, qg, kg, cos_t, sin_t, ones_bd, p_ctx_tiles, p_lat_tiles)

        wbd = _block_diag(pool_w[l]).astype(BF16)
        pool_out, xbc_act = _token_shift(seq_info, pool_in, xbc, wbd, pool_scale[l][None], conv_w[l], conv_b[l][None])

        lam_init = 0.8 - 0.6 * math.exp(-0.3 * l)
        lq = lambda_q[l].astype(F32)
        lk = lambda_k[l].astype(F32)
        lam = (jnp.exp(jnp.sum(lq[0] * lk[0])) - jnp.exp(jnp.sum(lq[1] * lk[1])) + lam_init).reshape(1)
        vb = v.astype(BF16)
        kT_ctx = kT[:, :n_ctx].reshape(QK_W, nb_ctx, t_ctx).transpose(1, 0, 2)
        v_ctx = vb[:n_ctx].reshape(nb_ctx, t_ctx, N_ATT_HEADS, ATT_V_DIM).transpose(0, 2, 1, 3)
        kT_lat = kT[:, n_ctx:].reshape(QK_W, nb_lat, t_lat).transpose(1, 0, 2)
        ck = cache_k[:, l].reshape(nb_lat, past, QK_W).transpose(0, 2, 1).astype(BF16)
        kT_lat = jnp.concatenate([kT_lat, ck], axis=2)
        v_lat = vb[n_ctx:].reshape(nb_lat, t_lat, N_ATT_HEADS, ATT_V_DIM).transpose(0, 2, 1, 3)
        cv = cache_v[:, l].astype(BF16).transpose(0, 2, 1, 3)
        v_lat = jnp.concatenate([v_lat, cv], axis=2)
        sg = subln_g[l][None]
        att_ctx = _attention(lam, qn, kT_ctx, v_ctx, sg, 0, nb_ctx, t_ctx, lam_init)
        att_lat = _attention(lam, qn, kT_lat, v_lat, sg, n_ctx, nb_lat, t_lat, lam_init)

        h0T = jnp.concatenate([
            jnp.zeros((nb_ctx, N_DIRS, N_SSM_HEADS, SSM_STATE, SSM_HEAD_DIM), F32),
            state_ssm[:, l].astype(F32).transpose(0, 1, 2, 4, 3)], axis=0)
        dtb = dt_bias[l].reshape(1, N_DIRS * N_SSM_HEADS)
        ac = (-jnp.exp(a_log[l].astype(F32))).reshape(1, N_DIRS * N_SSM_HEADS)
        dsk = jnp.repeat(d_skip[l].astype(F32), SSM_HEAD_DIM)[None]
        yf, yb, hfinT = _ssd(sched, xbc_act, dt, dtT, dtb, dtb.T, ac, ac.T, dsk, h0T, n_seq)

        x1, h2, eids, ewts = _out_projection(
            x, pool_out, att_ctx, att_lat, yf, yb, z, ssm_norm_g[l][None], w_out[l].astype(BF16), mod_l, norm2_g[l][None],
            w_router[l].T, router_bias[l].astype(F32)[:, None], p_ctx_tiles, p_lat_tiles)

        meta, src_rows, dst_rows, w_list = _dispatch_plan(eids, ewts)
        n_tiles = n // MOE_TS
        x_rows = h2.reshape(n_tiles, MOE_TS * SUBLANES, LANES)
        w_gateup = jnp.concatenate([w_gate[l].astype(BF16), w_up[l].astype(BF16)], axis=-1)
        acc = _routed_experts(meta, src_rows, dst_rows, w_list, x_rows, w_gateup, w_down[l].astype(BF16), n_tiles)
        x = _ffn_out(x1, h2, acc, ws_gate[l].astype(BF16), ws_up[l].astype(BF16), ws_down[l].astype(BF16),
                     mod_l, p_ctx_tiles, p_lat_tiles)

        new_k.append(kn[:n_ctx].reshape(nb_ctx, t_ctx, N_ATT_HEADS, 2, ATT_QK_DIM))
        new_v.append(v[:n_ctx].reshape(nb_ctx, t_ctx, N_ATT_HEADS, ATT_V_DIM))
        new_ssm.append(hfinT[:nb_ctx].transpose(0, 1, 2, 4, 3))

    y_prompt = x[:n_ctx].reshape(nb_ctx, t_ctx, d)
    y_sample = x[n_ctx:].reshape(nb_lat, t_lat, d)
    return (y_prompt, y_sample, jnp.stack(new_k, axis=1), jnp.stack(new_v, axis=1), jnp.stack(new_ssm, axis=1))
```
